```python
import math
import jax, jax.numpy as jnp
from jax import lax
import numpy as np

D_MODEL = 1024
BATCH = 8
SEQ = 4096
DEPTH = 2

MLA_HEADS = 8
MLA_NOPE = 64
MLA_ROPE = 32
MLA_QK = MLA_NOPE + MLA_ROPE
MLA_V = 64
MLA_Q_RANK = 384
MLA_KV_RANK = 256
MLA_DIM = MLA_HEADS * MLA_V
ROPE_THETA = 10000.0
ATTN_BLOCK = 128

RWKV_HEAD = 64
RWKV_HEADS = 8
RWKV_DIM = RWKV_HEADS * RWKV_HEAD
RWKV_W_RANK = 64
RWKV_A_RANK = 64
RWKV_V_RANK = 32
RWKV_G_RANK = 128
RWKV_GN_EPS = 64e-5

SSM_HEAD = 64
SSM_HEADS = 16
SSM_DIM = SSM_HEADS * SSM_HEAD
SSM_GROUPS = 2
SSM_HPG = SSM_HEADS // SSM_GROUPS
SSM_STATE = 128
SSM_CONV = 4
SSM_CHUNK = 256
SSM_CONV_DIM = SSM_DIM + 2 * SSM_GROUPS * SSM_STATE
SSM_NORM_EPS = 1e-5
DT_MIN = 1e-3
DT_MAX = 1e-1

N_BRANCH = 3
D_FF = 4 * D_MODEL
NORM_EPS = 1e-6
MAX_POS_OFFSET = 4096

MLA_IN = MLA_Q_RANK + MLA_KV_RANK + MLA_ROPE
RWKV_IN = 3 * RWKV_DIM + RWKV_W_RANK + RWKV_A_RANK + RWKV_G_RANK
SSM_IN = SSM_DIM + SSM_CONV_DIM + SSM_HEADS
GATE_IN = N_BRANCH * D_MODEL
IN_DIM = MLA_IN + RWKV_IN + SSM_IN + GATE_IN

kernel_name = 'hybrid_mla_rwkv7_mamba2_block'


def rms_norm(x, gain, eps):
    xf = x.astype(jnp.float32)
    y = xf * lax.rsqrt(jnp.mean(xf * xf, axis=-1, keepdims=True) + eps)
    return (y * gain.astype(jnp.float32)).astype(x.dtype)


def token_shift(t):
    return jnp.pad(t[:, :-1], ((0, 0), (1, 0), (0, 0)))


def rope(t, positions):
    half = t.shape[-1] // 2
    inv_freq = ROPE_THETA ** (-jnp.arange(half, dtype=jnp.float32) / half)
    ang = positions.astype(jnp.float32)[:, :, None, None] * inv_freq
    cos, sin = jnp.cos(ang), jnp.sin(ang)
    tf = t.astype(jnp.float32)
    t1, t2 = tf[..., :half], tf[..., half:]
    return jnp.concatenate([t1 * cos - t2 * sin, t1 * sin + t2 * cos], axis=-1).astype(t.dtype)


def causal_block_attention(q, k, v):
    seq = q.shape[1]
    scale = q.shape[-1] ** -0.5
    outs = []
    for i in range(seq // ATTN_BLOCK):
        lo, hi = i * ATTN_BLOCK, (i + 1) * ATTN_BLOCK
        s = jnp.einsum('bqhd,bkhd->bhqk', q[:, lo:hi], k[:, :hi], preferred_element_type=jnp.float32) * scale
        mask = (lo + jnp.arange(ATTN_BLOCK))[:, None] >= jnp.arange(hi)[None, :]
        s = jnp.where(mask, s, -jnp.inf)
        probs = jax.nn.softmax(s, axis=-1).astype(v.dtype)
        outs.append(jnp.einsum('bhqk,bkhd->bqhd', probs, v[:, :hi]))
    return jnp.concatenate(outs, axis=1)


def mla_branch(p, positions, q_norm_g, kv_norm_g, w_uq, w_ukv, q_head_g, k_head_g):
    bsz, seq, _ = p.shape
    c_q, c_kv, k_rope = jnp.split(p, [MLA_Q_RANK, MLA_Q_RANK + MLA_KV_RANK], axis=-1)
    c_q = rms_norm(c_q, q_norm_g, NORM_EPS)
    c_kv = rms_norm(c_kv, kv_norm_g, NORM_EPS)
    q = (c_q @ w_uq).reshape(bsz, seq, MLA_HEADS, MLA_QK)
    kv = (c_kv @ w_ukv).reshape(bsz, seq, MLA_HEADS, MLA_NOPE + MLA_V)
    k_nope, v = kv[..., :MLA_NOPE], kv[..., MLA_NOPE:]
    k_rope = jnp.broadcast_to(k_rope[:, :, None, :], (bsz, seq, MLA_HEADS, MLA_ROPE))
    k = jnp.concatenate([k_nope, k_rope], axis=-1)
    q = rms_norm(q, q_head_g, NORM_EPS)
    k = rms_norm(k, k_head_g, NORM_EPS)
    q = jnp.concatenate([q[..., :MLA_NOPE], rope(q[..., MLA_NOPE:], positions)], axis=-1)
    k = jnp.concatenate([k[..., :MLA_NOPE], rope(k[..., MLA_NOPE:], positions)], axis=-1)
    o = causal_block_attention(q, k, v)
    return o.reshape(bsz, seq, MLA_DIM)


def rwkv7_scan(r, decay, k, v, a, b):
    bsz, _, nh, n = r.shape

    def step(state, inp):
        r_t, w_t, k_t, v_t, a_t, b_t = inp
        sa = jnp.einsum('bhvk,bhk->bhv', state, a_t)
        state = state * w_t[:, :, None, :] + sa[..., None] * b_t[:, :, None, :] + v_t[..., None] * k_t[:, :, None, :]
        return state, jnp.einsum('bhvk,bhk->bhv', state, r_t)

    xs = tuple(jnp.swapaxes(t, 0, 1) for t in (r, decay, k, v, a, b))
    state0 = jnp.zeros((bsz, nh, n, n), jnp.float32)
    _, ys = lax.scan(step, state0, xs)
    return jnp.swapaxes(ys, 0, 1)


def rwkv7_branch(p, v_first, v_res, mu, w0, w2, a0, a2, g2, k_k, k_a, r_k, ln_g, ln_b):
    bsz, seq, _ = p.shape
    f32 = jnp.float32
    p = p + (token_shift(p) - p) * mu
    r, k, v, xw, xa, xg = jnp.split(
        p, [RWKV_DIM, 2 * RWKV_DIM, 3 * RWKV_DIM, 3 * RWKV_DIM + RWKV_W_RANK,
            3 * RWKV_DIM + RWKV_W_RANK + RWKV_A_RANK], axis=-1)
    w = -jax.nn.softplus(-(w0 + jnp.tanh(xw) @ w2).astype(f32)) - 0.5
    decay = jnp.exp(-jnp.exp(w))
    if v_res is None:
        v_first = v
    else:
        v0, v1, v2 = v_res
        v = v + (v_first - v) * jax.nn.sigmoid(v0 + (v @ v1) @ v2)
    a = jax.nn.sigmoid(a0 + xa @ a2)
    g = jax.nn.sigmoid(xg) @ g2

    def heads(t):
        return t.reshape(bsz, seq, RWKV_HEADS, RWKV_HEAD).astype(f32)

    kk = heads(k * k_k)
    kk = kk / jnp.maximum(jnp.sqrt(jnp.sum(kk * kk, axis=-1, keepdims=True)), 1e-12)
    k = k * (1.0 + (a - 1.0) * k_a)
    rh, kh, vh, ah = heads(r), heads(k), heads(v), heads(a)
    y = rwkv7_scan(rh, heads(decay), kh, vh, -kk, kk * ah)
    mean = jnp.mean(y, axis=-1, keepdims=True)
    var = jnp.mean(jnp.square(y - mean), axis=-1, keepdims=True)
    y = (y - mean) * lax.rsqrt(var + RWKV_GN_EPS)
    y = y * ln_g.astype(f32).reshape(RWKV_HEADS, RWKV_HEAD) + ln_b.astype(f32).reshape(RWKV_HEADS, RWKV_HEAD)
    y = y + jnp.sum(rh * kh * r_k.astype(f32), axis=-1, keepdims=True) * vh
    y = y.reshape(bsz, seq, RWKV_DIM) * g.astype(f32)
    return y.astype(p.dtype), v_first


def causal_depthwise_conv(x, w, b):
    width, ch = w.shape
    y = lax.conv_general_dilated(
        x, w[:, None, :], window_strides=(1,), padding=[(width - 1, 0)],
        dimension_numbers=('NWC', 'WIO', 'NWC'), feature_group_count=ch)
    return y + b


def segsum(a):
    t = a.shape[-1]
    rep = jnp.broadcast_to(a[..., :, None], a.shape + (t,))
    strict = jnp.tril(jnp.ones((t, t), dtype=bool), k=-1)
    cs = jnp.cumsum(jnp.where(strict, rep, 0.0), axis=-2)
    return jnp.where(jnp.tril(jnp.ones((t, t), dtype=bool)), cs, -jnp.inf)


def pad_seq(t, pad):
    return jnp.pad(t, [(0, 0), (0, pad)] + [(0, 0)] * (t.ndim - 2))


def ssd_chunked(x, da, bm, cm):
    bsz, seq = x.shape[:2]
    pad = (-seq) % SSM_CHUNK
    if pad:
        x, da, bm, cm = pad_seq(x, pad), pad_seq(da, pad), pad_seq(bm, pad), pad_seq(cm, pad)
    nc = (seq + pad) // SSM_CHUNK
    x = x.reshape(bsz, nc, SSM_CHUNK, SSM_GROUPS, SSM_HPG, SSM_HEAD)
    bm = bm.reshape(bsz, nc, SSM_CHUNK, SSM_GROUPS, SSM_STATE)
    cm = cm.reshape(bsz, nc, SSM_CHUNK, SSM_GROUPS, SSM_STATE)
    da = da.reshape(bsz, nc, SSM_CHUNK, SSM_GROUPS, SSM_HPG).transpose(0, 3, 4, 1, 2)
    a_cs = jnp.cumsum(da, axis=-1)
    decay_in = jnp.exp(segsum(da))
    cb = jnp.einsum('bclgn,bcsgn->bgcls', cm, bm)
    y_diag = jnp.einsum('bgcls,bgecls,bcsgep->bclgep', cb, decay_in, x)
    decay_to_end = jnp.exp(a_cs[..., -1:] - a_cs)
    states = jnp.einsum('bcsgn,bgecs,bcsgep->bcgepn', bm, decay_to_end, x)
    chunk_tot = jnp.pad(a_cs[..., -1], ((0, 0), (0, 0), (0, 0), (1, 0)))
    decay_chunk = jnp.exp(segsum(chunk_tot))
    states = jnp.pad(states, ((0, 0), (1, 0), (0, 0), (0, 0), (0, 0), (0, 0)))
    states = jnp.einsum('bgezc,bcgepn->bzgepn', decay_chunk, states)[:, :-1]
    y_off = jnp.einsum('bclgn,bcgepn,bgecl->bclgep', cm, states, jnp.exp(a_cs))
    y = (y_diag + y_off).reshape(bsz, nc * SSM_CHUNK, SSM_GROUPS, SSM_HPG, SSM_HEAD)
    return y[:, :seq]


def mamba2_branch(p, conv_w, conv_b, dt_bias, a_log, d_skip, norm_g):
    bsz, seq, _ = p.shape
    f32 = jnp.float32
    z, xbc, dt = jnp.split(p, [SSM_DIM, SSM_DIM + SSM_CONV_DIM], axis=-1)
    xbc = jax.nn.silu(causal_depthwise_conv(xbc, conv_w, conv_b))
    xs, b_in, c_in = jnp.split(xbc, [SSM_DIM, SSM_DIM + SSM_GROUPS * SSM_STATE], axis=-1)
    xs = xs.reshape(bsz, seq, SSM_GROUPS, SSM_HPG, SSM_HEAD).astype(f32)
    b_in = b_in.reshape(bsz, seq, SSM_GROUPS, SSM_STATE).astype(f32)
    c_in = c_in.reshape(bsz, seq, SSM_GROUPS, SSM_STATE).astype(f32)
    dt = jax.nn.softplus(dt.astype(f32) + dt_bias.astype(f32)).reshape(bsz, seq, SSM_GROUPS, SSM_HPG)
    a = -jnp.exp(a_log.astype(f32)).reshape(SSM_GROUPS, SSM_HPG)
    y = ssd_chunked(xs * dt[..., None], dt * a, b_in, c_in)
    y = y + xs * d_skip.astype(f32).reshape(SSM_GROUPS, SSM_HPG)[..., None]
    y = y.reshape(bsz, seq, SSM_DIM) * jax.nn.silu(z.astype(f32))
    y = y.reshape(bsz, seq, SSM_GROUPS, SSM_DIM // SSM_GROUPS)
    y = y * lax.rsqrt(jnp.mean(y * y, axis=-1, keepdims=True) + SSM_NORM_EPS)
    y = y.reshape(bsz, seq, SSM_DIM) * norm_g.astype(f32)
    return y.astype(p.dtype)


def setup_inputs(seed: int = 0) -> dict:
    key = jax.random.key(seed)
    ks = iter(list(jax.random.split(key, 64)))
    L = DEPTH

    def nrm(shape, scale):
        return jax.random.normal(next(ks), shape, jnp.float32) * scale

    def gain(shape):
        return 1.0 + nrm(shape, 0.02)

    def unif(shape, lo, hi):
        return jax.random.uniform(next(ks), shape, jnp.float32, minval=lo, maxval=hi)

    x = nrm((BATCH, SEQ, D_MODEL), 1.0)
    start = jax.random.randint(next(ks), (BATCH, 1), 0, MAX_POS_OFFSET, dtype=jnp.int32)
    positions = start + jnp.arange(SEQ, dtype=jnp.int32)[None, :]

    dt = jnp.exp(unif((L, SSM_HEADS), math.log(DT_MIN), math.log(DT_MAX)))
    dt_bias = dt + jnp.log(-jnp.expm1(-dt))

    return {
        'x': x,
        'positions': positions,
        'norm_mix_g': gain((L, D_MODEL)),
        'w_in': nrm((L, D_MODEL, IN_DIM), D_MODEL ** -0.5),
        'mla_q_norm_g': gain((L, MLA_Q_RANK)),
        'mla_kv_norm_g': gain((L, MLA_KV_RANK)),
        'mla_w_uq': nrm((L, MLA_Q_RANK, MLA_HEADS * MLA_QK), MLA_Q_RANK ** -0.5),
        'mla_w_ukv': nrm((L, MLA_KV_RANK, MLA_HEADS * (MLA_NOPE + MLA_V)), MLA_KV_RANK ** -0.5),
        'mla_q_head_g': gain((L, MLA_QK)),
        'mla_k_head_g': gain((L, MLA_QK)),
        'rwkv_mu': unif((L, RWKV_IN), 0.0, 1.0),
        'rwkv_w0': unif((L, RWKV_DIM), -6.5, -1.5),
        'rwkv_w2': nrm((L, RWKV_W_RANK, RWKV_DIM), RWKV_W_RANK ** -0.5),
        'rwkv_a0': nrm((L, RWKV_DIM), 0.1),
        'rwkv_a2': nrm((L, RWKV_A_RANK, RWKV_DIM), RWKV_A_RANK ** -0.5),
        'rwkv_g2': nrm((L, RWKV_G_RANK, RWKV_DIM), RWKV_G_RANK ** -0.5),
        'rwkv_v0': 1.0 + nrm((L - 1, RWKV_DIM), 0.1),
        'rwkv_v1': nrm((L - 1, RWKV_DIM, RWKV_V_RANK), RWKV_DIM ** -0.5),
        'rwkv_v2': nrm((L - 1, RWKV_V_RANK, RWKV_DIM), RWKV_V_RANK ** -0.5),
        'rwkv_k_k': 0.85 + nrm((L, RWKV_DIM), 0.02),
        'rwkv_k_a': gain((L, RWKV_DIM)),
        'rwkv_r_k': nrm((L, RWKV_HEADS, RWKV_HEAD), 0.1),
        'rwkv_ln_g': gain((L, RWKV_DIM)),
        'rwkv_ln_b': nrm((L, RWKV_DIM), 0.02),
        'ssm_conv_w': nrm((L, SSM_CONV, SSM_CONV_DIM), SSM_CONV ** -0.5),
        'ssm_conv_b': nrm((L, SSM_CONV_DIM), 0.02),
        'ssm_dt_bias': dt_bias,
        'ssm_a_log': jnp.log(unif((L, SSM_HEADS), 1.0, 16.0)),
        'ssm_d': 1.0 + nrm((L, SSM_HEADS), 0.1),
        'ssm_norm_g': gain((L, SSM_DIM)),
        'w_br_mla': nrm((L, MLA_DIM, D_MODEL), MLA_DIM ** -0.5),
        'w_br_rwkv': nrm((L, RWKV_DIM, D_MODEL), RWKV_DIM ** -0.5),
        'w_br_ssm': nrm((L, SSM_DIM, D_MODEL), SSM_DIM ** -0.5),
        'w_out': nrm((L, D_MODEL, D_MODEL), D_MODEL ** -0.5),
        'norm_ffn_g': gain((L, D_MODEL)),
        'w_ff1': nrm((L, D_MODEL, D_FF), D_MODEL ** -0.5),
        'w_ff2': nrm((L, D_FF, D_MODEL), D_FF ** -0.5),
    }


def reference(x, positions, norm_mix_g, w_in, mla_q_norm_g, mla_kv_norm_g, mla_w_uq, mla_w_ukv,
              mla_q_head_g, mla_k_head_g, rwkv_mu, rwkv_w0, rwkv_w2, rwkv_a0, rwkv_a2, rwkv_g2,
              rwkv_v0, rwkv_v1, rwkv_v2, rwkv_k_k, rwkv_k_a, rwkv_r_k, rwkv_ln_g, rwkv_ln_b,
              ssm_conv_w, ssm_conv_b, ssm_dt_bias, ssm_a_log, ssm_d, ssm_norm_g,
              w_br_mla, w_br_rwkv, w_br_ssm, w_out, norm_ffn_g, w_ff1, w_ff2):
    bsz, seq, _ = x.shape
    splits = [MLA_IN, MLA_IN + RWKV_IN, MLA_IN + RWKV_IN + SSM_IN]
    v_first = None
    for l in range(DEPTH):
        h = rms_norm(x, norm_mix_g[l], NORM_EPS)
        proj = h @ w_in[l]
        p_mla, p_rwkv, p_ssm, p_gate = jnp.split(proj, splits, axis=-1)

        o_mla = mla_branch(p_mla, positions, mla_q_norm_g[l], mla_kv_norm_g[l], mla_w_uq[l],
                           mla_w_ukv[l], mla_q_head_g[l], mla_k_head_g[l])
        v_res = None if l == 0 else (rwkv_v0[l - 1], rwkv_v1[l - 1], rwkv_v2[l - 1])
        o_rwkv, v_first = rwkv7_branch(p_rwkv, v_first, v_res, rwkv_mu[l], rwkv_w0[l], rwkv_w2[l],
                                       rwkv_a0[l], rwkv_a2[l], rwkv_g2[l], rwkv_k_k[l], rwkv_k_a[l],
                                       rwkv_r_k[l], rwkv_ln_g[l], rwkv_ln_b[l])
        o_ssm = mamba2_branch(p_ssm, ssm_conv_w[l], ssm_conv_b[l], ssm_dt_bias[l], ssm_a_log[l],
                              ssm_d[l], ssm_norm_g[l])

        gates = jax.nn.sigmoid(p_gate.astype(jnp.float32)).astype(x.dtype)
        gates = gates.reshape(bsz, seq, N_BRANCH, D_MODEL)
        merged = (gates[:, :, 0] * (o_mla @ w_br_mla[l])
                  + gates[:, :, 1] * (o_rwkv @ w_br_rwkv[l])
                  + gates[:, :, 2] * (o_ssm @ w_br_ssm[l]))
        x = x + merged @ w_out[l]

        h = rms_norm(x, norm_ffn_g[l], NORM_EPS)
        x = x + jnp.square(jax.nn.relu(h @ w_ff1[l])) @ w_ff2[l]
    return x
```

```python
import functools

import jax
import jax.numpy as jnp
from jax import lax
from jax.experimental import pallas as pl
from jax.experimental.pallas import tpu as pltpu

F32 = jnp.float32
BF16 = jnp.bfloat16

D_MODEL = 1024
MLA_HEADS = 8
MLA_NOPE = 64
MLA_ROPE = 32
MLA_QK = MLA_NOPE + MLA_ROPE
MLA_V = 64
MLA_Q_RANK = 384
MLA_KV_RANK = 256
ROPE_THETA = 10000.0
RWKV_HEAD = 64
RWKV_HEADS = 8
RWKV_DIM = RWKV_HEADS * RWKV_HEAD
RWKV_W_RANK = 64
RWKV_A_RANK = 64
RWKV_V_RANK = 32
RWKV_G_RANK = 128
RWKV_GN_EPS = 64e-5
SSM_HEAD = 64
SSM_HEADS = 16
SSM_DIM = SSM_HEADS * SSM_HEAD
SSM_GROUPS = 2
SSM_HPG = SSM_HEADS // SSM_GROUPS
SSM_STATE = 128
SSM_CONV = 4
SSM_CHUNK = 256
SSM_CONV_DIM = SSM_DIM + 2 * SSM_GROUPS * SSM_STATE
SSM_NORM_EPS = 1e-5
N_BRANCH = 3
D_FF = 4 * D_MODEL
NORM_EPS = 1e-6
MLA_IN = MLA_Q_RANK + MLA_KV_RANK + MLA_ROPE
RWKV_IN = 3 * RWKV_DIM + RWKV_W_RANK + RWKV_A_RANK + RWKV_G_RANK
SSM_IN = SSM_DIM + SSM_CONV_DIM + SSM_HEADS
GATE_IN = N_BRANCH * D_MODEL

LANE = 128
SUBLANE = 8
VMEM_LIMIT = 56 * 1024 * 1024

TOK_TILE = 512
FFN_TOK_TILE = 1024
FFN_FF_TILE = 1024
ATTN_TILE = 512
RWKV_CHUNK = 64
RWKV_TILE = 512


def _dot(a, b):
    return jnp.dot(a.astype(BF16), b.astype(BF16), preferred_element_type=F32)


def _dot_nt(a, b):
    return lax.dot_general(a.astype(BF16), b.astype(BF16), (((1,), (1,)), ((), ())),
                           preferred_element_type=F32)


def _dot_exact(a, b):
    return jnp.dot(a, b, preferred_element_type=F32, precision=lax.Precision.HIGHEST)


def _dot_nt_exact(a, b):
    return lax.dot_general(a, b, (((1,), (1,)), ((), ())), preferred_element_type=F32,
                           precision=lax.Precision.HIGHEST)


def _rms(x, g, eps):
    return x * lax.rsqrt(jnp.mean(x * x, axis=-1, keepdims=True) + eps) * g


def _sigmoid(x):
    return 1.0 / (1.0 + jnp.exp(-x))


def _softplus(x):
    return jnp.maximum(x, 0.0) + jnp.log(1.0 + jnp.exp(-jnp.abs(x)))


def _iota2(shape, dim):
    return lax.broadcasted_iota(jnp.int32, shape, dim)


def _params(*sem):
    return pltpu.CompilerParams(dimension_semantics=sem, vmem_limit_bytes=VMEM_LIMIT)


def _full(shape):
    return pl.BlockSpec(shape, lambda *_: (0,) * len(shape))


def _rope_table_kernel(pos_ref, freq_ref, cos_ref, sin_ref):
    ang = pos_ref[...].astype(F32) * freq_ref[...]
    cos_ref[...] = jnp.cos(ang)
    sin_ref[...] = jnp.sin(ang)


def _rope_tables(positions):
    half = MLA_ROPE // 2
    n_tok = positions.size
    inv_freq = ROPE_THETA ** (-jnp.arange(half, dtype=jnp.float32) / half)
    per_row = LANE // half
    rows = n_tok // per_row
    pos = jnp.repeat(positions.reshape(rows, per_row), half, axis=1)
    freq = jnp.tile(inv_freq, per_row).reshape(1, LANE)
    blk = min(rows, 512)
    cos, sin = pl.pallas_call(
        _rope_table_kernel,
        grid=(rows // blk,),
        in_specs=[pl.BlockSpec((blk, LANE), lambda i: (i, 0)), _full((1, LANE))],
        out_specs=[pl.BlockSpec((blk, LANE), lambda i: (i, 0))] * 2,
        out_shape=[jax.ShapeDtypeStruct((rows, LANE), F32)] * 2,
        compiler_params=_params("parallel"),
        name="rope_table",
    )(pos, freq)
    cos = cos.reshape(n_tok, half)
    sin = sin.reshape(n_tok, half)
    one = jnp.ones((n_tok, MLA_NOPE), F32)
    z_nope = jnp.zeros((n_tok, MLA_NOPE), F32)
    z_half = jnp.zeros((n_tok, half), F32)
    z_pad = jnp.zeros((n_tok, LANE - MLA_QK), F32)
    cosf = jnp.concatenate([one, cos, cos, z_pad], axis=1)
    sina = jnp.concatenate([z_nope, -sin, z_half, z_pad], axis=1)
    sinb = jnp.concatenate([z_nope, z_half, sin, z_pad], axis=1)
    return cosf, sina, sinb


def _mla_prep_kernel(x_ref, gmix_ref, wmla_ref, qg_ref, kvg_ref, wuq_ref, wuk_ref, wuv_ref,
                     qhg_ref, khg_ref, cos_ref, sina_ref, sinb_ref, q_ref, k_ref, v_ref):
    h = _rms(x_ref[...], gmix_ref[...], NORM_EPS)
    c = _dot(h, wmla_ref[...])
    cq = _rms(c[:, :MLA_Q_RANK], qg_ref[...], NORM_EPS)
    ckv = _rms(c[:, MLA_Q_RANK:MLA_Q_RANK + MLA_KV_RANK], kvg_ref[...], NORM_EPS)
    krope = c[:, MLA_Q_RANK + MLA_KV_RANK:]
    qf = _dot(cq, wuq_ref[...])
    kf = _dot(ckv, wuk_ref[...])
    v_ref[...] = _dot(ckv, wuv_ref[...]).astype(BF16)
    cosf, sina, sinb = cos_ref[...], sina_ref[...], sinb_ref[...]
    qhg, khg = qhg_ref[...], khg_ref[...]
    scale = MLA_QK ** -0.5

    def norm_rope(t, g):
        t = t * lax.rsqrt(jnp.sum(t * t, axis=-1, keepdims=True) * (1.0 / MLA_QK) + NORM_EPS) * g
        up = pltpu.roll(t, LANE - MLA_ROPE // 2, 1)
        down = pltpu.roll(t, MLA_ROPE // 2, 1)
        return t * cosf + up * sina + down * sinb

    for hd in range(MLA_HEADS):
        sl = slice(LANE * hd, LANE * (hd + 1))
        q_ref[:, sl] = (norm_rope(qf[:, sl], qhg) * scale).astype(BF16)
        k_ref[:, sl] = norm_rope(kf[:, sl] + krope, khg).astype(BF16)


def _attn_kernel(q_ref, k_ref, v_ref, o_ref):
    i = pl.program_id(2)
    tq = q_ref.shape[0]
    outs = []
    for hh in range(2):
        q = q_ref[:, LANE * hh:LANE * (hh + 1)]

        def step(j, carry, masked):
            m, l, acc = carry
            rows = pl.ds(pl.multiple_of(j * tq, tq), tq)
            k = k_ref[rows, LANE * hh:LANE * (hh + 1)]
            v = v_ref[rows, MLA_V * hh:MLA_V * (hh + 1)]
            s = _dot_nt(q, k)
            if masked:
                s = jnp.where(_iota2(s.shape, 0) >= _iota2(s.shape, 1), s, -jnp.inf)
            m_new = jnp.maximum(m, jnp.max(s, axis=-1, keepdims=True))
            p = jnp.exp(s - m_new)
            alpha = jnp.exp(m - m_new)
            l = alpha * l + jnp.sum(p, axis=-1, keepdims=True)
            acc = alpha * acc + _dot(p, v)
            return m_new, l, acc

        init = (jnp.full((tq, 1), -jnp.inf, F32), jnp.zeros((tq, 1), F32), jnp.zeros((tq, MLA_V), F32))
        carry = lax.fori_loop(0, i, functools.partial(step, masked=False), init)
        _, l, acc = step(i, carry, True)
        outs.append(acc / l)
    o_ref[...] = jnp.concatenate(outs, axis=-1).astype(BF16)


def _mla_branch(x2, bsz, seq, gmix, wp, tables):
    n_tok = x2.shape[0]
    tm = min(TOK_TILE, seq)
    cosf, sina, sinb = tables
    tok = lambda w: pl.BlockSpec((tm, w), lambda i: (i, 0))
    hq = MLA_HEADS * LANE
    q, k, v = pl.pallas_call(
        _mla_prep_kernel,
        grid=(n_tok // tm,),
        in_specs=[tok(D_MODEL), _full((1, D_MODEL)), _full(wp["w_mla"].shape), _full((1, MLA_Q_RANK)),
                  _full((1, MLA_KV_RANK)), _full(wp["w_uq"].shape), _full(wp["w_uk"].shape),
                  _full(wp["w_uv"].shape), _full((1, LANE)), _full((1, LANE)), tok(LANE), tok(LANE), tok(LANE)],
        out_specs=[tok(hq), tok(hq), tok(MLA_HEADS * MLA_V)],
        out_shape=[jax.ShapeDtypeStruct((n_tok, hq), BF16), jax.ShapeDtypeStruct((n_tok, hq), BF16),
                   jax.ShapeDtypeStruct((n_tok, MLA_HEADS * MLA_V), BF16)],
        compiler_params=_params("parallel"),
        name="mla_prep",
    )(x2, gmix, wp["w_mla"], wp["q_norm_g"], wp["kv_norm_g"], wp["w_uq"], wp["w_uk"], wp["w_uv"],
      wp["q_head_g"], wp["k_head_g"], cosf, sina, sinb)

    tq = min(ATTN_TILE, seq)
    nq = seq // tq
    return pl.pallas_call(
        _attn_kernel,
        grid=(bsz, MLA_HEADS // 2, nq),
        in_specs=[pl.BlockSpec((tq, 2 * LANE), lambda b, hp, i: (b * nq + i, hp)),
                  pl.BlockSpec((seq, 2 * LANE), lambda b, hp, i: (b, hp)),
                  pl.BlockSpec((seq, 2 * MLA_V), lambda b, hp, i: (b, hp))],
        out_specs=pl.BlockSpec((tq, 2 * MLA_V), lambda b, hp, i: (b * nq + i, hp)),
        out_shape=jax.ShapeDtypeStruct((n_tok, MLA_HEADS * MLA_V), BF16),
        compiler_params=_params("parallel", "parallel", "arbitrary"),
        name="mla_attention",
    )(q, k, v)


def _rwkv_prep_kernel(has_vres, *refs):
    if has_vres:
        (x_ref, gmix_ref, w_ref, mu_ref, w0_ref, w2_ref, a0_ref, a2_ref, g2_ref, kk_ref, ka_ref,
         vfirst_ref, v0_ref, v1_ref, v2_ref,
         r_out, lw_out, k_out, v_out, kk_out, a_out, g_out, carry_ref) = refs
    else:
        (x_ref, gmix_ref, w_ref, mu_ref, w0_ref, w2_ref, a0_ref, a2_ref, g2_ref, kk_ref, ka_ref,
         r_out, lw_out, k_out, v_out, kk_out, a_out, g_out, carry_ref) = refs

    @pl.when(pl.program_id(1) == 0)
    def _():
        carry_ref[...] = jnp.zeros_like(carry_ref)

    h = _rms(x_ref[...], gmix_ref[...], NORM_EPS)
    p = _dot(h, w_ref[...])
    tm = p.shape[0]
    rolled = pltpu.roll(p, 1, 0)
    prev = jnp.where(_iota2(p.shape, 0) == 0, carry_ref[...], rolled)
    carry_ref[...] = p[tm - 1:tm, :]
    p = p + (prev - p) * mu_ref[...]

    d = RWKV_DIM
    r, k, v = p[:, :d], p[:, d:2 * d], p[:, 2 * d:3 * d]
    xwa = p[:, 3 * d:3 * d + LANE]
    xg = p[:, 3 * d + LANE:]
    w = -_softplus(-(w0_ref[...] + _dot(jnp.tanh(xwa), w2_ref[...]))) - 0.5
    lw_out[...] = -jnp.exp(w)
    if has_vres:
        gate = _sigmoid(v0_ref[...] + _dot(_dot(v, v1_ref[...]), v2_ref[...]))
        v = v + (vfirst_ref[...] - v) * gate
    a = _sigmoid(a0_ref[...] + _dot(xwa, a2_ref[...]))
    g_out[...] = _dot(_sigmoid(xg), g2_ref[...])
    r_out[...] = r
    kk_out[...] = k * kk_ref[...]
    k_out[...] = k * (1.0 + (a - 1.0) * ka_ref[...])
    v_out[...] = v
    a_out[...] = a


def _rwkv_scan_kernel(r_ref, lw_ref, k_ref, v_ref, kk_ref, a_ref, g_ref, rk_ref, lng_ref, lnb_ref,
                      o_ref, state_ref):
    L = RWKV_CHUNK
    n = RWKV_HEAD

    @pl.when(pl.program_id(2) == 0)
    def _():
        state_ref[...] = jnp.zeros_like(state_ref)

    row = _iota2((L, L), 0)
    col = _iota2((L, L), 1)
    tril = (row >= col).astype(F32)
    lower = row >= col
    strict = row > col
    eye = (row == col).astype(F32)

    def chunk(c, _):
        rows = pl.ds(pl.multiple_of(c * L, L), L)
        outs = []
        for hh in range(2):
            sl = slice(n * hh, n * (hh + 1))
            r = r_ref[rows, sl]
            lw = lw_ref[rows, sl]
            k = k_ref[rows, sl]
            v = v_ref[rows, sl]
            kk = kk_ref[rows, sl]
            a = a_ref[rows, sl]
            st = state_ref[hh]

            kk = kk / jnp.maximum(jnp.sqrt(jnp.sum(kk * kk, axis=-1, keepdims=True)), 1e-12)
            av = -kk
            bv = kk * a
            cum = _dot_exact(tril, lw)
            tot = cum[L - 1:L, :]
            dec_in = jnp.exp(cum - lw)
            dec_out = jnp.exp(cum)
            inv = jnp.exp(-cum)
            to_end = jnp.exp(tot - cum)
            at = av * dec_in
            rt = r * dec_out
            bt = bv * inv
            kt = k * inv
            tot_col = jnp.sum(_dot_nt_exact(eye, lw), axis=-1, keepdims=True)

            mab = jnp.where(strict, _dot_nt(at, bt), 0.0)
            mak = jnp.where(strict, _dot_nt(at, kt), 0.0)
            nrb = jnp.where(lower, _dot_nt(rt, bt), 0.0)
            nrk = jnp.where(lower, _dot_nt(rt, kt), 0.0)
            tinv = eye + mab
            pw = mab
            for _ in range(5):
                pw = _dot_exact(pw, pw)
                tinv = tinv + _dot_exact(tinv, pw)

            u = _dot_exact(tinv, _dot(at, st) + _dot(mak, v))
            y = _dot(rt, st) + _dot(nrb, u) + _dot(nrk, v)
            bend_t = _dot_nt(eye, bv * to_end)
            kend_t = _dot_nt(eye, k * to_end)
            state_ref[hh] = st * jnp.exp(tot_col) + _dot(bend_t, u) + _dot(kend_t, v)

            mean = jnp.mean(y, axis=-1, keepdims=True)
            var = jnp.mean(jnp.square(y - mean), axis=-1, keepdims=True)
            y = (y - mean) * lax.rsqrt(var + RWKV_GN_EPS)
            y = y * lng_ref[:, sl] + lnb_ref[:, sl]
            y = y + jnp.sum(r * k * rk_ref[:, sl], axis=-1, keepdims=True) * v
            outs.append(y * g_ref[rows, sl])
        o_ref[rows, :] = jnp.concatenate(outs, axis=-1).astype(BF16)
        return 0

    lax.fori_loop(0, r_ref.shape[0] // L, chunk, 0)


def _rwkv_branch(x2, bsz, seq, gmix, wp, v_first):
    n_tok = x2.shape[0]
    tm = min(TOK_TILE, seq)
    ns = seq // tm
    has_vres = v_first is not None
    d = RWKV_DIM
    tok = lambda w: pl.BlockSpec((tm, w), lambda b, i: (b * ns + i, 0))
    vec = _full((1, d))
    in_specs = [tok(D_MODEL), _full((1, D_MODEL)), _full((D_MODEL, RWKV_IN)), _full((1, RWKV_IN)), vec,
                _full((LANE, d)), vec, _full((LANE, d)), _full((RWKV_G_RANK, d)), vec, vec]
    args = [x2, gmix, wp["w_rwkv"], wp["mu"], wp["w0"], wp["w2"], wp["a0"], wp["a2"], wp["g2"], wp["k_k"], wp["k_a"]]
    if has_vres:
        in_specs += [tok(d), vec, _full((d, LANE)), _full((LANE, d))]
        args += [v_first, wp["v0"], wp["v1"], wp["v2"]]
    r, lw, k, v, kk, a, g = pl.pallas_call(
        functools.partial(_rwkv_prep_kernel, has_vres),
        grid=(bsz, ns),
        in_specs=in_specs,
        out_specs=[tok(d)] * 7,
        out_shape=[jax.ShapeDtypeStruct((n_tok, d), F32)] * 7,
        scratch_shapes=[pltpu.VMEM((1, RWKV_IN), F32)],
        compiler_params=_params("parallel", "arbitrary"),
        name="rwkv_prep",
    )(*args)

    ts = min(RWKV_TILE, seq)
    nt = seq // ts
    blk = pl.BlockSpec((ts, LANE), lambda b, hp, i: (b * nt + i, hp))
    par = pl.BlockSpec((1, LANE), lambda b, hp, i: (0, hp))
    o = pl.pallas_call(
        _rwkv_scan_kernel,
        grid=(bsz, RWKV_HEADS // 2, nt),
        in_specs=[blk] * 7 + [par] * 3,
        out_specs=blk,
        out_shape=jax.ShapeDtypeStruct((n_tok, d), BF16),
        scratch_shapes=[pltpu.VMEM((2, RWKV_HEAD, RWKV_HEAD), F32)],
        compiler_params=_params("parallel", "parallel", "arbitrary"),
        name="rwkv_scan",
    )(r, lw, k, v, kk, a, g, wp["r_k"], wp["ln_g"], wp["ln_b"])
    return o, v


def _ssm_prep_kernel(x_ref, gmix_ref, w_ref, cw_ref, cb_ref, dtb_ref,
                     z_out, xs_out, b_out, c_out, dt_out, carry_ref):
    @pl.when(pl.program_id(1) == 0)
    def _():
        carry_ref[...] = jnp.zeros_like(carry_ref)

    h = _rms(x_ref[...], gmix_ref[...], NORM_EPS)
    p = _dot(h, w_ref[...])
    z_out[...] = p[:, :SSM_DIM]
    xbc = p[:, SSM_DIM:SSM_DIM + SSM_CONV_DIM]
    dt_out[...] = _softplus(p[:, SSM_DIM + SSM_CONV_DIM:] + dtb_ref[...])

    tm = xbc.shape[0]
    prev = carry_ref[...]
    carry_ref[...] = xbc[tm - SUBLANE:, :]
    top_row = _iota2((SUBLANE, SSM_CONV_DIM), 0)
    acc = xbc * cw_ref[SSM_CONV - 1:SSM_CONV, :] + cb_ref[...]
    for s in range(1, SSM_CONV):
        rolled = pltpu.roll(xbc, s, 0)
        top = jnp.where(top_row < s, pltpu.roll(prev, s, 0), rolled[:SUBLANE])
        shifted = jnp.concatenate([top, rolled[SUBLANE:]], axis=0)
        acc = acc + shifted * cw_ref[SSM_CONV - 1 - s:SSM_CONV - s, :]
    act = acc * _sigmoid(acc)
    xs_out[...] = act[:, :SSM_DIM]
    gn = SSM_GROUPS * SSM_STATE
    b_out[...] = act[:, SSM_DIM:SSM_DIM + gn].astype(BF16)
    c_out[...] = act[:, SSM_DIM + gn:].astype(BF16)


def _ssd_kernel(xs_ref, b_ref, c_ref, dt_ref, z_ref, alog_ref, d_ref, ng_ref, o_ref, state_ref, y_ref):
    @pl.when(pl.program_id(1) == 0)
    def _():
        state_ref[...] = jnp.zeros_like(state_ref)

    L = xs_ref.shape[0]
    row = _iota2((L, L), 0)
    col = _iota2((L, L), 1)
    lower = row >= col
    tril = lower.astype(F32)
    eye = (_iota2((LANE, LANE), 0) == _iota2((LANE, LANE), 1)).astype(F32)

    dt = dt_ref[...]
    da = dt * (-jnp.exp(alog_ref[...]))
    acs = _dot_exact(tril, da)
    acs_t = _dot_nt_exact(eye, acs)
    tot = acs[L - 1:L, :]
    decay_out = jnp.exp(acs)
    decay_end = jnp.exp(tot - acs)
    decay_tot = jnp.exp(tot)

    for g in range(SSM_GROUPS):
        gs = slice(SSM_STATE * g, SSM_STATE * (g + 1))
        bm = b_ref[:, gs]
        cm = c_ref[:, gs]
        cb = _dot_nt(cm, bm)
        bm_t = _dot_nt(eye, bm)
        for e in range(SSM_HPG):
            hd = SSM_HPG * g + e
            hs = slice(SSM_HEAD * hd, SSM_HEAD * (hd + 1))
            xs = xs_ref[:, hs]
            xdt = xs * dt[:, hd:hd + 1]
            seg = jnp.where(lower, jnp.exp(acs[:, hd:hd + 1] - acs_t[hd:hd + 1, :]), 0.0)
            st = state_ref[hd]
            y = _dot(cb * seg, xdt) + _dot(cm, st) * decay_out[:, hd:hd + 1]
            state_ref[hd] = st * decay_tot[:, hd:hd + 1] + _dot(bm_t, xdt * decay_end[:, hd:hd + 1])
            y_ref[:, hs] = y + xs * d_ref[:, hd:hd + 1]

    z = z_ref[...]
    y = y_ref[...] * (z * _sigmoid(z))
    gw = SSM_DIM // SSM_GROUPS
    for g in range(SSM_GROUPS):
        gs = slice(gw * g, gw * (g + 1))
        yg = y[:, gs]
        yg = yg * lax.rsqrt(jnp.mean(yg * yg, axis=-1, keepdims=True) + SSM_NORM_EPS)
        o_ref[:, gs] = (yg * ng_ref[:, gs]).astype(BF16)


def _ssm_branch(x2, bsz, seq, gmix, wp):
    n_tok = x2.shape[0]
    tm = min(TOK_TILE, seq)
    ns = seq // tm
    gn = SSM_GROUPS * SSM_STATE
    tok = lambda w: pl.BlockSpec((tm, w), lambda b, i: (b * ns + i, 0))
    z, xs, bm, cm, dt = pl.pallas_call(
        _ssm_prep_kernel,
        grid=(bsz, ns),
        in_specs=[tok(D_MODEL), _full((1, D_MODEL)), _full(wp["w_ssm"].shape), _full((SSM_CONV, SSM_CONV_DIM)),
                  _full((1, SSM_CONV_DIM)), _full((1, LANE))],
        out_specs=[tok(SSM_DIM), tok(SSM_DIM), tok(gn), tok(gn), tok(LANE)],
        out_shape=[jax.ShapeDtypeStruct((n_tok, SSM_DIM), F32), jax.ShapeDtypeStruct((n_tok, SSM_DIM), F32),
                   jax.ShapeDtypeStruct((n_tok, gn), BF16), jax.ShapeDtypeStruct((n_tok, gn), BF16),
                   jax.ShapeDtypeStruct((n_tok, LANE), F32)],
        scratch_shapes=[pltpu.VMEM((SUBLANE, SSM_CONV_DIM), F32)],
        compiler_params=_params("parallel", "arbitrary"),
        name="ssm_prep",
    )(x2, gmix, wp["w_ssm"], wp["conv_w"], wp["conv_b"], wp["dt_bias"])

    assert seq % SSM_CHUNK == 0
    nc = seq // SSM_CHUNK
    ch = lambda w: pl.BlockSpec((SSM_CHUNK, w), lambda b, c: (b * nc + c, 0))
    return pl.pallas_call(
        _ssd_kernel,
        grid=(bsz, nc),
        in_specs=[ch(SSM_DIM), ch(gn), ch(gn), ch(LANE), ch(SSM_DIM), _full((1, LANE)), _full((1, LANE)),
                  _full((1, SSM_DIM))],
        out_specs=ch(SSM_DIM),
        out_shape=jax.ShapeDtypeStruct((n_tok, SSM_DIM), BF16),
        scratch_shapes=[pltpu.VMEM((SSM_HEADS, SSM_STATE, SSM_HEAD), F32), pltpu.VMEM((SSM_CHUNK, SSM_DIM), F32)],
        compiler_params=_params("parallel", "arbitrary"),
        name="ssd",
    )(xs, bm, cm, dt, z, wp["a_log"], wp["d_skip"], wp["norm_g"])


def _merge_kernel(x_ref, gmix_ref, wg_ref, om_ref, or_ref, os_ref, wm_ref, wr_ref, ws_ref, wo_ref, o_ref):
    x = x_ref[...]
    h = _rms(x, gmix_ref[...], NORM_EPS).astype(BF16)
    d = D_MODEL
    merged = _sigmoid(_dot(h, wg_ref[:, :d])) * _dot(om_ref[...], wm_ref[...])
    merged = merged + _sigmoid(_dot(h, wg_ref[:, d:2 * d])) * _dot(or_ref[...], wr_ref[...])
    merged = merged + _sigmoid(_dot(h, wg_ref[:, 2 * d:])) * _dot(os_ref[...], ws_ref[...])
    o_ref[...] = x + _dot(merged, wo_ref[...])


def _ffn_kernel(x_ref, g_ref, w1_ref, w2_ref, o_ref, h_ref):
    j = pl.program_id(1)

    @pl.when(j == 0)
    def _():
        x = x_ref[...]
        h_ref[...] = _rms(x, g_ref[...], NORM_EPS).astype(BF16)
        o_ref[...] = x

    u = jnp.maximum(_dot(h_ref[...], w1_ref[...]), 0.0)
    o_ref[...] += _dot(u * u, w2_ref[...])


def _merge_ffn(x2, gmix, wp, o_mla, o_rwkv, o_ssm):
    n_tok = x2.shape[0]
    tm = min(TOK_TILE, n_tok)
    tok = lambda w: pl.BlockSpec((tm, w), lambda i: (i, 0))
    x2 = pl.pallas_call(
        _merge_kernel,
        grid=(n_tok // tm,),
        in_specs=[tok(D_MODEL), _full((1, D_MODEL)), _full((D_MODEL, GATE_IN)), tok(o_mla.shape[1]),
                  tok(o_rwkv.shape[1]), tok(o_ssm.shape[1]), _full(wp["w_br_mla"].shape),
                  _full(wp["w_br_rwkv"].shape), _full(wp["w_br_ssm"].shape), _full((D_MODEL, D_MODEL))],
        out_specs=tok(D_MODEL),
        out_shape=jax.ShapeDtypeStruct((n_tok, D_MODEL), F32),
        compiler_params=_params("parallel"),
        name="merge",
    )(x2, gmix, wp["w_gate"], o_mla, o_rwkv, o_ssm, wp["w_br_mla"], wp["w_br_rwkv"], wp["w_br_ssm"], wp["w_out"])

    tf = min(FFN_TOK_TILE, n_tok)
    ff = FFN_FF_TILE
    return pl.pallas_call(
        _ffn_kernel,
        grid=(n_tok // tf, D_FF // ff),
        in_specs=[pl.BlockSpec((tf, D_MODEL), lambda i, j: (i, 0)), _full((1, D_MODEL)),
                  pl.BlockSpec((D_MODEL, ff), lambda i, j: (0, j)), pl.BlockSpec((ff, D_MODEL), lambda i, j: (j, 0))],
        out_specs=pl.BlockSpec((tf, D_MODEL), lambda i, j: (i, 0)),
        out_shape=jax.ShapeDtypeStruct((n_tok, D_MODEL), F32),
        scratch_shapes=[pltpu.VMEM((tf, D_MODEL), BF16)],
        compiler_params=_params("parallel", "arbitrary"),
        name="ffn",
    )(x2, wp["norm_ffn_g"], wp["w_ff1"], wp["w_ff2"])


def _pad_cols(w, width):
    return jnp.pad(w, ((0, 0), (0, width - w.shape[1])))


def _pad_rows(w, height, before=0):
    return jnp.pad(w, ((before, height - before - w.shape[0]), (0, 0)))


def _row(v, width=None):
    v = v.reshape(1, -1).astype(F32)
    return v if width is None else _pad_cols(v, width)


def _layer_weights(l, p):
    w_in = p["w_in"][l]
    o_r = MLA_IN
    o_s = o_r + RWKV_IN
    o_g = o_s + SSM_IN
    zc = lambda n: jnp.zeros((D_MODEL, n), w_in.dtype)
    w_mla = jnp.concatenate([w_in[:, :MLA_Q_RANK + MLA_KV_RANK], zc(MLA_NOPE), w_in[:, MLA_Q_RANK + MLA_KV_RANK:MLA_IN],
                             zc(LANE - MLA_QK)], axis=1)
    w_uq = p["mla_w_uq"][l].reshape(MLA_Q_RANK, MLA_HEADS, MLA_QK)
    w_uq = jnp.pad(w_uq, ((0, 0), (0, 0), (0, LANE - MLA_QK))).reshape(MLA_Q_RANK, MLA_HEADS * LANE)
    w_ukv = p["mla_w_ukv"][l].reshape(MLA_KV_RANK, MLA_HEADS, MLA_NOPE + MLA_V)
    w_uk = jnp.pad(w_ukv[:, :, :MLA_NOPE], ((0, 0), (0, 0), (0, LANE - MLA_NOPE))).reshape(MLA_KV_RANK, MLA_HEADS * LANE)
    w_uv = w_ukv[:, :, MLA_NOPE:].reshape(MLA_KV_RANK, MLA_HEADS * MLA_V)
    w_ssm = _pad_cols(w_in[:, o_s:o_g], SSM_DIM + SSM_CONV_DIM + LANE)
    wp = {
        "w_mla": w_mla.astype(BF16), "w_uq": w_uq.astype(BF16), "w_uk": w_uk.astype(BF16), "w_uv": w_uv.astype(BF16),
        "q_norm_g": _row(p["mla_q_norm_g"][l]), "kv_norm_g": _row(p["mla_kv_norm_g"][l]),
        "q_head_g": _row(p["mla_q_head_g"][l], LANE), "k_head_g": _row(p["mla_k_head_g"][l], LANE),
        "w_rwkv": w_in[:, o_r:o_s].astype(BF16), "mu": _row(p["rwkv_mu"][l]), "w0": _row(p["rwkv_w0"][l]),
        "w2": _pad_rows(p["rwkv_w2"][l], LANE).astype(BF16), "a0": _row(p["rwkv_a0"][l]),
        "a2": _pad_rows(p["rwkv_a2"][l], LANE, before=RWKV_W_RANK).astype(BF16), "g2": p["rwkv_g2"][l].astype(BF16),
        "k_k": _row(p["rwkv_k_k"][l]), "k_a": _row(p["rwkv_k_a"][l]), "r_k": _row(p["rwkv_r_k"][l]),
        "ln_g": _row(p["rwkv_ln_g"][l]), "ln_b": _row(p["rwkv_ln_b"][l]),
        "w_ssm": w_ssm.astype(BF16), "conv_w": p["ssm_conv_w"][l].astype(F32), "conv_b": _row(p["ssm_conv_b"][l]),
        "dt_bias": _row(p["ssm_dt_bias"][l], LANE), "a_log": _row(p["ssm_a_log"][l], LANE),
        "d_skip": _row(p["ssm_d"][l], LANE), "norm_g": _row(p["ssm_norm_g"][l]),
        "w_gate": w_in[:, o_g:].astype(BF16), "w_br_mla": p["w_br_mla"][l].astype(BF16),
        "w_br_rwkv": p["w_br_rwkv"][l].astype(BF16), "w_br_ssm": p["w_br_ssm"][l].astype(BF16),
        "w_out": p["w_out"][l].astype(BF16), "norm_ffn_g": _row(p["norm_ffn_g"][l]),
        "w_ff1": p["w_ff1"][l].astype(BF16), "w_ff2": p["w_ff2"][l].astype(BF16),
    }
    if l > 0:
        wp["v0"] = _row(p["rwkv_v0"][l - 1])
        wp["v1"] = _pad_cols(p["rwkv_v1"][l - 1], LANE).astype(BF16)
        wp["v2"] = _pad_rows(p["rwkv_v2"][l - 1], LANE).astype(BF16)
    return wp


def kernel(x, positions, norm_mix_g, w_in, mla_q_norm_g, mla_kv_norm_g, mla_w_uq, mla_w_ukv, mla_q_head_g, mla_k_head_g, rwkv_mu, rwkv_w0, rwkv_w2, rwkv_a0, rwkv_a2, rwkv_g2, rwkv_v0, rwkv_v1, rwkv_v2, rwkv_k_k, rwkv_k_a, rwkv_r_k, rwkv_ln_g, rwkv_ln_b, ssm_conv_w, ssm_conv_b, ssm_dt_bias, ssm_a_log, ssm_d, ssm_norm_g, w_br_mla, w_br_rwkv, w_br_ssm, w_out, norm_ffn_g, w_ff1, w_ff2):
    p = dict(w_in=w_in, mla_q_norm_g=mla_q_norm_g, mla_kv_norm_g=mla_kv_norm_g, mla_w_uq=mla_w_uq,
             mla_w_ukv=mla_w_ukv, mla_q_head_g=mla_q_head_g, mla_k_head_g=mla_k_head_g, rwkv_mu=rwkv_mu,
             rwkv_w0=rwkv_w0, rwkv_w2=rwkv_w2, rwkv_a0=rwkv_a0, rwkv_a2=rwkv_a2, rwkv_g2=rwkv_g2, rwkv_v0=rwkv_v0,
             rwkv_v1=rwkv_v1, rwkv_v2=rwkv_v2, rwkv_k_k=rwkv_k_k, rwkv_k_a=rwkv_k_a, rwkv_r_k=rwkv_r_k,
             rwkv_ln_g=rwkv_ln_g, rwkv_ln_b=rwkv_ln_b, ssm_conv_w=ssm_conv_w, ssm_conv_b=ssm_conv_b,
             ssm_dt_bias=ssm_dt_bias, ssm_a_log=ssm_a_log, ssm_d=ssm_d, ssm_norm_g=ssm_norm_g, w_br_mla=w_br_mla,
             w_br_rwkv=w_br_rwkv, w_br_ssm=w_br_ssm, w_out=w_out, norm_ffn_g=norm_ffn_g, w_ff1=w_ff1, w_ff2=w_ff2)
    bsz, seq, d_model = x.shape
    depth = w_in.shape[0]
    x2 = x.reshape(bsz * seq, d_model)
    tables = _rope_tables(positions)
    v_first = None
    for l in range(depth):
        wp = _layer_weights(l, p)
        gmix = _row(norm_mix_g[l])
        o_mla = _mla_branch(x2, bsz, seq, gmix, wp, tables)
        o_rwkv, v = _rwkv_branch(x2, bsz, seq, gmix, wp, v_first)
        if l == 0:
            v_first = v
        o_ssm = _ssm_branch(x2, bsz, seq, gmix, wp)
        x2 = _merge_ffn(x2, gmix, wp, o_mla, o_rwkv, o_ssm)
    return x2.reshape(bsz, seq, d_model)
```

```python
import functools

import jax
import jax.numpy as jnp
from jax import lax
from jax.experimental import pallas as pl
from jax.experimental.pallas import tpu as pltpu

F32 = jnp.float32
BF16 = jnp.bfloat16

D_MODEL = 1024
MLA_HEADS = 8
MLA_NOPE = 64
MLA_ROPE = 32
MLA_QK = MLA_NOPE + MLA_ROPE
MLA_V = 64
MLA_Q_RANK = 384
MLA_KV_RANK = 256
ROPE_THETA = 10000.0
RWKV_HEAD = 64
RWKV_HEADS = 8
RWKV_DIM = RWKV_HEADS * RWKV_HEAD
RWKV_W_RANK = 64
RWKV_A_RANK = 64
RWKV_V_RANK = 32
RWKV_G_RANK = 128
RWKV_GN_EPS = 64e-5
SSM_HEAD = 64
SSM_HEADS = 16
SSM_DIM = SSM_HEADS * SSM_HEAD
SSM_GROUPS = 2
SSM_HPG = SSM_HEADS // SSM_GROUPS
SSM_STATE = 128
SSM_CONV = 4
SSM_CHUNK = 256
SSM_CONV_DIM = SSM_DIM + 2 * SSM_GROUPS * SSM_STATE
SSM_NORM_EPS = 1e-5
N_BRANCH = 3
D_FF = 4 * D_MODEL
NORM_EPS = 1e-6
MLA_IN = MLA_Q_RANK + MLA_KV_RANK + MLA_ROPE
RWKV_IN = 3 * RWKV_DIM + RWKV_W_RANK + RWKV_A_RANK + RWKV_G_RANK
SSM_IN = SSM_DIM + SSM_CONV_DIM + SSM_HEADS
GATE_IN = N_BRANCH * D_MODEL

LANE = 128
SUBLANE = 8
VMEM_LIMIT = 56 * 1024 * 1024

TOK_TILE = 512
FFN_TOK_TILE = 1024
FFN_FF_TILE = 1024
ATTN_TILE = 512
RWKV_CHUNK = 64
RWKV_TILE = 512


def _dot(a, b):
    return jnp.dot(a.astype(BF16), b.astype(BF16), preferred_element_type=F32)


def _dot_nt(a, b):
    return lax.dot_general(a.astype(BF16), b.astype(BF16), (((1,), (1,)), ((), ())),
                           preferred_element_type=F32)


def _dot_exact(a, b):
    return jnp.dot(a, b, preferred_element_type=F32, precision=lax.Precision.HIGHEST)


def _dot_nt_exact(a, b):
    return lax.dot_general(a, b, (((1,), (1,)), ((), ())), preferred_element_type=F32,
                           precision=lax.Precision.HIGHEST)


def _rms(x, g, eps):
    return x * lax.rsqrt(jnp.mean(x * x, axis=-1, keepdims=True) + eps) * g


def _sigmoid(x):
    return 1.0 / (1.0 + jnp.exp(-x))


def _softplus(x):
    return jnp.maximum(x, 0.0) + jnp.log(1.0 + jnp.exp(-jnp.abs(x)))


def _iota2(shape, dim):
    return lax.broadcasted_iota(jnp.int32, shape, dim)


def _params(*sem):
    return pltpu.CompilerParams(dimension_semantics=sem, vmem_limit_bytes=VMEM_LIMIT)


def _full(shape):
    return pl.BlockSpec(shape, lambda *_: (0,) * len(shape))


def _rope_table_kernel(pos_ref, freq_ref, cos_ref, sin_ref):
    ang = pos_ref[...].astype(F32) * freq_ref[...]
    cos_ref[...] = jnp.cos(ang)
    sin_ref[...] = jnp.sin(ang)


def _rope_tables(positions):
    half = MLA_ROPE // 2
    n_tok = positions.size
    inv_freq = ROPE_THETA ** (-jnp.arange(half, dtype=jnp.float32) / half)
    per_row = LANE // half
    rows = n_tok // per_row
    pos = jnp.repeat(positions.reshape(rows, per_row), half, axis=1)
    freq = jnp.tile(inv_freq, per_row).reshape(1, LANE)
    blk = min(rows, 512)
    cos, sin = pl.pallas_call(
        _rope_table_kernel,
        grid=(rows // blk,),
        in_specs=[pl.BlockSpec((blk, LANE), lambda i: (i, 0)), _full((1, LANE))],
        out_specs=[pl.BlockSpec((blk, LANE), lambda i: (i, 0))] * 2,
        out_shape=[jax.ShapeDtypeStruct((rows, LANE), F32)] * 2,
        compiler_params=_params("parallel"),
        name="rope_table",
    )(pos, freq)
    cos = cos.reshape(n_tok, half)
    sin = sin.reshape(n_tok, half)
    one = jnp.ones((n_tok, MLA_NOPE), F32)
    z_nope = jnp.zeros((n_tok, MLA_NOPE), F32)
    z_half = jnp.zeros((n_tok, half), F32)
    z_pad = jnp.zeros((n_tok, LANE - MLA_QK), F32)
    cosf = jnp.concatenate([one, cos, cos, z_pad], axis=1)
    sina = jnp.concatenate([z_nope, -sin, z_half, z_pad], axis=1)
    sinb = jnp.concatenate([z_nope, z_half, sin, z_pad], axis=1)
    return cosf, sina, sinb


def _mla_prep_kernel(x_ref, gmix_ref, wmla_ref, qg_ref, kvg_ref, wuq_ref, wuk_ref, wuv_ref,
                     qhg_ref, khg_ref, cos_ref, sina_ref, sinb_ref, q_ref, k_ref, v_ref):
    h = _rms(x_ref[...], gmix_ref[...], NORM_EPS)
    c = _dot(h, wmla_ref[...])
    cq = _rms(c[:, :MLA_Q_RANK], qg_ref[...], NORM_EPS)
    ckv = _rms(c[:, MLA_Q_RANK:MLA_Q_RANK + MLA_KV_RANK], kvg_ref[...], NORM_EPS)
    krope = c[:, MLA_Q_RANK + MLA_KV_RANK:]
    qf = _dot(cq, wuq_ref[...])
    kf = _dot(ckv, wuk_ref[...])
    v_ref[...] = _dot(ckv, wuv_ref[...]).astype(BF16)
    cosf, sina, sinb = cos_ref[...], sina_ref[...], sinb_ref[...]
    qhg, khg = qhg_ref[...], khg_ref[...]
    scale = MLA_QK ** -0.5

    def norm_rope(t, g):
        t = t * lax.rsqrt(jnp.sum(t * t, axis=-1, keepdims=True) * (1.0 / MLA_QK) + NORM_EPS) * g
        up = pltpu.roll(t, LANE - MLA_ROPE // 2, 1)
        down = pltpu.roll(t, MLA_ROPE // 2, 1)
        return t * cosf + up * sina + down * sinb

    for hd in range(MLA_HEADS):
        sl = slice(LANE * hd, LANE * (hd + 1))
        q_ref[:, sl] = (norm_rope(qf[:, sl], qhg) * scale).astype(BF16)
        k_ref[:, sl] = norm_rope(kf[:, sl] + krope, khg).astype(BF16)


def _attn_kernel(q_ref, k_ref, v_ref, o_ref):
    i = pl.program_id(2)
    tq = q_ref.shape[0]
    outs = []
    for hh in range(2):
        q = q_ref[:, LANE * hh:LANE * (hh + 1)]

        def step(j, carry, masked):
            m, l, acc = carry
            rows = pl.ds(pl.multiple_of(j * tq, tq), tq)
            k = k_ref[rows, LANE * hh:LANE * (hh + 1)]
            v = v_ref[rows, MLA_V * hh:MLA_V * (hh + 1)]
            s = _dot_nt(q, k)
            if masked:
                s = jnp.where(_iota2(s.shape, 0) >= _iota2(s.shape, 1), s, -jnp.inf)
            m_new = jnp.maximum(m, jnp.max(s, axis=-1, keepdims=True))
            p = jnp.exp(s - m_new)
            alpha = jnp.exp(m - m_new)
            l = alpha * l + jnp.sum(p, axis=-1, keepdims=True)
            acc = alpha * acc + _dot(p, v)
            return m_new, l, acc

        init = (jnp.full((tq, 1), -jnp.inf, F32), jnp.zeros((tq, 1), F32), jnp.zeros((tq, MLA_V), F32))
        carry = lax.fori_loop(0, i, functools.partial(step, masked=False), init)
        _, l, acc = step(i, carry, True)
        outs.append(acc / l)
    o_ref[...] = jnp.concatenate(outs, axis=-1).astype(BF16)


def _mla_branch(x2, bsz, seq, gmix, wp, tables):
    n_tok = x2.shape[0]
    tm = min(TOK_TILE, seq)
    cosf, sina, sinb = tables
    tok = lambda w: pl.BlockSpec((tm, w), lambda i: (i, 0))
    hq = MLA_HEADS * LANE
    q, k, v = pl.pallas_call(
        _mla_prep_kernel,
        grid=(n_tok // tm,),
        in_specs=[tok(D_MODEL), _full((1, D_MODEL)), _full(wp["w_mla"].shape), _full((1, MLA_Q_RANK)),
                  _full((1, MLA_KV_RANK)), _full(wp["w_uq"].shape), _full(wp["w_uk"].shape),
                  _full(wp["w_uv"].shape), _full((1, LANE)), _full((1, LANE)), tok(LANE), tok(LANE), tok(LANE)],
        out_specs=[tok(hq), tok(hq), tok(MLA_HEADS * MLA_V)],
        out_shape=[jax.ShapeDtypeStruct((n_tok, hq), BF16), jax.ShapeDtypeStruct((n_tok, hq), BF16),
                   jax.ShapeDtypeStruct((n_tok, MLA_HEADS * MLA_V), BF16)],
        compiler_params=_params("parallel"),
        name="mla_prep",
    )(x2, gmix, wp["w_mla"], wp["q_norm_g"], wp["kv_norm_g"], wp["w_uq"], wp["w_uk"], wp["w_uv"],
      wp["q_head_g"], wp["k_head_g"], cosf, sina, sinb)

    tq = min(ATTN_TILE, seq)
    nq = seq // tq
    return pl.pallas_call(
        _attn_kernel,
        grid=(bsz, MLA_HEADS // 2, nq),
        in_specs=[pl.BlockSpec((tq, 2 * LANE), lambda b, hp, i: (b * nq + i, hp)),
                  pl.BlockSpec((seq, 2 * LANE), lambda b, hp, i: (b, hp)),
                  pl.BlockSpec((seq, 2 * MLA_V), lambda b, hp, i: (b, hp))],
        out_specs=pl.BlockSpec((tq, 2 * MLA_V), lambda b, hp, i: (b * nq + i, hp)),
        out_shape=jax.ShapeDtypeStruct((n_tok, MLA_HEADS * MLA_V), BF16),
        compiler_params=_params("parallel", "parallel", "arbitrary"),
        name="mla_attention",
    )(q, k, v)


def _rwkv_prep_kernel(has_vres, *refs):
    if has_vres:
        (x_ref, gmix_ref, w_ref, mu_ref, w0_ref, w2_ref, a0_ref, a2_ref, g2_ref, kk_ref, ka_ref,
         vfirst_ref, v0_ref, v1_ref, v2_ref,
         r_out, lw_out, k_out, v_out, kk_out, a_out, g_out, carry_ref) = refs
    else:
        (x_ref, gmix_ref, w_ref, mu_ref, w0_ref, w2_ref, a0_ref, a2_ref, g2_ref, kk_ref, ka_ref,
         r_out, lw_out, k_out, v_out, kk_out, a_out, g_out, carry_ref) = refs

    @pl.when(pl.program_id(1) == 0)
    def _():
        carry_ref[...] = jnp.zeros_like(carry_ref)

    h = _rms(x_ref[...], gmix_ref[...], NORM_EPS)
    p = _dot(h, w_ref[...])
    tm = p.shape[0]
    rolled = pltpu.roll(p, 1, 0)
    prev = jnp.where(_iota2(p.shape, 0) == 0, carry_ref[...], rolled)
    carry_ref[...] = p[tm - 1:tm, :]
    p = p + (prev - p) * mu_ref[...]

    d = RWKV_DIM
    r, k, v = p[:, :d], p[:, d:2 * d], p[:, 2 * d:3 * d]
    xwa = p[:, 3 * d:3 * d + LANE]
    xg = p[:, 3 * d + LANE:]
    w = -_softplus(-(w0_ref[...] + _dot(jnp.tanh(xwa), w2_ref[...]))) - 0.5
    lw_out[...] = -jnp.exp(w)
    if has_vres:
        gate = _sigmoid(v0_ref[...] + _dot(_dot(v, v1_ref[...]), v2_ref[...]))
        v = v + (vfirst_ref[...] - v) * gate
    a = _sigmoid(a0_ref[...] + _dot(xwa, a2_ref[...]))
    g_out[...] = _dot(_sigmoid(xg), g2_ref[...])
    r_out[...] = r
    kk_out[...] = k * kk_ref[...]
    k_out[...] = k * (1.0 + (a - 1.0) * ka_ref[...])
    v_out[...] = v
    a_out[...] = a


def _bmm(a, b):
    return lax.dot_general(a.astype(BF16), b.astype(BF16), (((2,), (1,)), ((0,), (0,))),
                           preferred_element_type=F32)


def _bmm_nt(a, b):
    return lax.dot_general(a.astype(BF16), b.astype(BF16), (((2,), (2,)), ((0,), (0,))),
                           preferred_element_type=F32)


def _bmm_tn(a, b):
    return lax.dot_general(a.astype(BF16), b.astype(BF16), (((1,), (1,)), ((0,), (0,))),
                           preferred_element_type=F32)


def _dot_tn(a, b):
    return lax.dot_general(a.astype(BF16), b.astype(BF16), (((0,), (0,)), ((), ())),
                           preferred_element_type=F32)


def _head_sum(x, first):
    s0 = jnp.sum(jnp.where(first, x, 0.0), axis=-1, keepdims=True)
    s1 = jnp.sum(jnp.where(first, 0.0, x), axis=-1, keepdims=True)
    return jnp.where(first, s0, s1)


def _rwkv_scan_kernel(r_ref, lw_ref, k_ref, v_ref, kk_ref, a_ref, g_ref, rk_ref, lng_ref, lnb_ref,
                      o_ref, state_ref, y_ref):
    L = RWKV_CHUNK
    n = RWKV_HEAD
    ts = r_ref.shape[0]
    nchunk = ts // L
    sh3 = (nchunk, L, LANE)

    @pl.when(pl.program_id(2) == 0)
    def _():
        state_ref[...] = jnp.zeros_like(state_ref)

    first = _iota2(sh3, 2) < n
    t_idx = _iota2(sh3, 1)
    r = r_ref[...].reshape(sh3)
    lw = lw_ref[...].reshape(sh3)
    k = k_ref[...].reshape(sh3)
    v = v_ref[...].reshape(sh3)
    kk = kk_ref[...].reshape(sh3)
    a = a_ref[...].reshape(sh3)

    kk = kk / jnp.maximum(jnp.sqrt(_head_sum(kk * kk, first)), 1e-12)
    cum = lw
    s = 1
    while s < L:
        cum = cum + jnp.where(t_idx >= s, pltpu.roll(cum, s, 1), 0.0)
        s *= 2
    tot = cum[:, L - 1:L, :]
    inv = jnp.exp(-cum)
    to_end = jnp.exp(tot - cum)
    at3 = -kk * jnp.exp(cum - lw)
    rt3 = r * jnp.exp(cum)
    bt3 = kk * a * inv
    kt3 = k * inv
    bend3 = kk * a * to_end
    kend3 = k * to_end
    lw_hi = lw.astype(BF16)
    lw_r = lw - lw_hi.astype(F32)
    lw_mid = lw_r.astype(BF16)
    lw_lo = (lw_r - lw_mid.astype(F32)).astype(BF16)

    row = _iota2((nchunk, L, L), 1)
    col = _iota2((nchunk, L, L), 2)
    lower = row >= col
    strict = row > col
    eye = (row == col).astype(F32)
    ones = jnp.ones((nchunk, L, n), BF16)

    pre = []
    for hh in range(2):
        sl = slice(n * hh, n * (hh + 1))
        at, rt, bt, kt = at3[:, :, sl], rt3[:, :, sl], bt3[:, :, sl], kt3[:, :, sl]
        vh = v[:, :, sl]
        ar = jnp.concatenate([at, rt], axis=1)
        pb = _bmm_nt(ar, bt)
        pk = _bmm_nt(ar, kt)
        mab = jnp.where(strict, pb[:, :L], 0.0)
        nrb = jnp.where(lower, pb[:, L:], 0.0)
        mak = jnp.where(strict, pk[:, :L], 0.0)
        nrk = jnp.where(lower, pk[:, L:], 0.0)
        tinv = eye + mab
        pw = mab
        s = 2
        while s < L:
            pw = _bmm(pw, pw)
            tinv = tinv + _bmm(tinv, pw)
            s *= 2
        abar = _bmm(tinv, at)
        u0 = _bmm(tinv, _bmm(mak, vh))
        y0 = _bmm(nrk, vh)
        kv0 = _bmm_tn(kend3[:, :, sl], vh)
        tot_col = (_bmm_tn(lw_hi[:, :, sl], ones) + _bmm_tn(lw_mid[:, :, sl], ones)
                   + _bmm_tn(lw_lo[:, :, sl], ones))
        pre.append((jnp.concatenate([abar, rt], axis=1).astype(BF16), u0, nrb.astype(BF16), y0,
                    bend3[:, :, sl].astype(BF16), kv0, jnp.exp(tot_col)))

    for c in range(nchunk):
        for hh in range(2):
            ar, u0, nrb, y0, bend, kv0, gam = pre[hh]
            st = state_ref[hh]
            rd = _dot(ar[c], st)
            u = rd[:L] + u0[c]
            y_ref[L * c:L * (c + 1), n * hh:n * (hh + 1)] = rd[L:] + _dot(nrb[c], u) + y0[c]
            state_ref[hh] = st * gam[c] + _dot_tn(bend[c], u) + kv0[c]

    first2 = _iota2((ts, LANE), 1) < n
    y = y_ref[...]
    mean = _head_sum(y, first2) * (1.0 / n)
    var = _head_sum(jnp.square(y - mean), first2) * (1.0 / n)
    y = (y - mean) * lax.rsqrt(var + RWKV_GN_EPS)
    y = y * lng_ref[...] + lnb_ref[...]
    r2, k2, v2 = r_ref[...], k_ref[...], v_ref[...]
    y = y + _head_sum(r2 * k2 * rk_ref[...], first2) * v2
    o_ref[...] = (y * g_ref[...]).astype(BF16)


def _rwkv_scan_kernel_seq(r_ref, lw_ref, k_ref, v_ref, kk_ref, a_ref, g_ref, rk_ref, lng_ref, lnb_ref,
                          o_ref, state_ref):
    L = RWKV_CHUNK
    n = RWKV_HEAD

    @pl.when(pl.program_id(2) == 0)
    def _():
        state_ref[...] = jnp.zeros_like(state_ref)

    row = _iota2((L, L), 0)
    col = _iota2((L, L), 1)
    tril = (row >= col).astype(F32)
    lower = row >= col
    strict = row > col
    eye = (row == col).astype(F32)

    def chunk(c, _):
        rows = pl.ds(pl.multiple_of(c * L, L), L)
        outs = []
        for hh in range(2):
            sl = slice(n * hh, n * (hh + 1))
            r = r_ref[rows, sl]
            lw = lw_ref[rows, sl]
            k = k_ref[rows, sl]
            v = v_ref[rows, sl]
            kk = kk_ref[rows, sl]
            a = a_ref[rows, sl]
            st = state_ref[hh]

            kk = kk / jnp.maximum(jnp.sqrt(jnp.sum(kk * kk, axis=-1, keepdims=True)), 1e-12)
            av = -kk
            bv = kk * a
            cum = _dot_exact(tril, lw)
            tot = cum[L - 1:L, :]
            dec_in = jnp.exp(cum - lw)
            dec_out = jnp.exp(cum)
            inv = jnp.exp(-cum)
            to_end = jnp.exp(tot - cum)
            at = av * dec_in
            rt = r * dec_out
            bt = bv * inv
            kt = k * inv
            tot_col = jnp.sum(_dot_nt_exact(eye, lw), axis=-1, keepdims=True)

            mab = jnp.where(strict, _dot_nt(at, bt), 0.0)
            mak = jnp.where(strict, _dot_nt(at, kt), 0.0)
            nrb = jnp.where(lower, _dot_nt(rt, bt), 0.0)
            nrk = jnp.where(lower, _dot_nt(rt, kt), 0.0)
            tinv = eye + mab
            pw = mab
            for _ in range(5):
                pw = _dot_exact(pw, pw)
                tinv = tinv + _dot_exact(tinv, pw)

            u = _dot_exact(tinv, _dot(at, st) + _dot(mak, v))
            y = _dot(rt, st) + _dot(nrb, u) + _dot(nrk, v)
            bend_t = _dot_nt(eye, bv * to_end)
            kend_t = _dot_nt(eye, k * to_end)
            state_ref[hh] = st * jnp.exp(tot_col) + _dot(bend_t, u) + _dot(kend_t, v)

            mean = jnp.mean(y, axis=-1, keepdims=True)
            var = jnp.mean(jnp.square(y - mean), axis=-1, keepdims=True)
            y = (y - mean) * lax.rsqrt(var + RWKV_GN_EPS)
            y = y * lng_ref[:, sl] + lnb_ref[:, sl]
            y = y + jnp.sum(r * k * rk_ref[:, sl], axis=-1, keepdims=True) * v
            outs.append(y * g_ref[rows, sl])
        o_ref[rows, :] = jnp.concatenate(outs, axis=-1).astype(BF16)
        return 0

    lax.fori_loop(0, r_ref.shape[0] // L, chunk, 0)


def _rwkv_branch(x2, bsz, seq, gmix, wp, v_first):
    n_tok = x2.shape[0]
    tm = min(TOK_TILE, seq)
    ns = seq // tm
    has_vres = v_first is not None
    d = RWKV_DIM
    tok = lambda w: pl.BlockSpec((tm, w), lambda b, i: (b * ns + i, 0))
    vec = _full((1, d))
    in_specs = [tok(D_MODEL), _full((1, D_MODEL)), _full((D_MODEL, RWKV_IN)), _full((1, RWKV_IN)), vec,
                _full((LANE, d)), vec, _full((LANE, d)), _full((RWKV_G_RANK, d)), vec, vec]
    args = [x2, gmix, wp["w_rwkv"], wp["mu"], wp["w0"], wp["w2"], wp["a0"], wp["a2"], wp["g2"], wp["k_k"], wp["k_a"]]
    if has_vres:
        in_specs += [tok(d), vec, _full((d, LANE)), _full((LANE, d))]
        args += [v_first, wp["v0"], wp["v1"], wp["v2"]]
    r, lw, k, v, kk, a, g = pl.pallas_call(
        functools.partial(_rwkv_prep_kernel, has_vres),
        grid=(bsz, ns),
        in_specs=in_specs,
        out_specs=[tok(d)] * 7,
        out_shape=[jax.ShapeDtypeStruct((n_tok, d), F32)] * 7,
        scratch_shapes=[pltpu.VMEM((1, RWKV_IN), F32)],
        compiler_params=_params("parallel", "arbitrary"),
        name="rwkv_prep",
    )(*args)

    ts = min(RWKV_TILE, seq)
    nt = seq // ts
    blk = pl.BlockSpec((ts, LANE), lambda b, hp, i: (b * nt + i, hp))
    par = pl.BlockSpec((1, LANE), lambda b, hp, i: (0, hp))
    o = pl.pallas_call(
        _rwkv_scan_kernel,
        grid=(bsz, RWKV_HEADS // 2, nt),
        in_specs=[blk] * 7 + [par] * 3,
        out_specs=blk,
        out_shape=jax.ShapeDtypeStruct((n_tok, d), BF16),
        scratch_shapes=[pltpu.VMEM((2, RWKV_HEAD, RWKV_HEAD), F32), pltpu.VMEM((ts, LANE), F32)],
        compiler_params=_params("parallel", "parallel", "arbitrary"),
        name="rwkv_scan",
    )(r, lw, k, v, kk, a, g, wp["r_k"], wp["ln_g"], wp["ln_b"])
    return o, v


def _ssm_prep_kernel(x_ref, gmix_ref, w_ref, cw_ref, cb_ref, dtb_ref,
                     z_out, xs_out, b_out, c_out, dt_out, carry_ref):
    @pl.when(pl.program_id(1) == 0)
    def _():
        carry_ref[...] = jnp.zeros_like(carry_ref)

    h = _rms(x_ref[...], gmix_ref[...], NORM_EPS)
    p = _dot(h, w_ref[...])
    z_out[...] = p[:, :SSM_DIM]
    xbc = p[:, SSM_DIM:SSM_DIM + SSM_CONV_DIM]
    dt_out[...] = _softplus(p[:, SSM_DIM + SSM_CONV_DIM:] + dtb_ref[...])

    tm = xbc.shape[0]
    prev = carry_ref[...]
    carry_ref[...] = xbc[tm - SUBLANE:, :]
    top_row = _iota2((SUBLANE, SSM_CONV_DIM), 0)
    acc = xbc * cw_ref[SSM_CONV - 1:SSM_CONV, :] + cb_ref[...]
    for s in range(1, SSM_CONV):
        rolled = pltpu.roll(xbc, s, 0)
        top = jnp.where(top_row < s, pltpu.roll(prev, s, 0), rolled[:SUBLANE])
        shifted = jnp.concatenate([top, rolled[SUBLANE:]], axis=0)
        acc = acc + shifted * cw_ref[SSM_CONV - 1 - s:SSM_CONV - s, :]
    act = acc * _sigmoid(acc)
    xs_out[...] = act[:, :SSM_DIM]
    gn = SSM_GROUPS * SSM_STATE
    b_out[...] = act[:, SSM_DIM:SSM_DIM + gn].astype(BF16)
    c_out[...] = act[:, SSM_DIM + gn:].astype(BF16)


def _ssd_kernel(xs_ref, b_ref, c_ref, dt_ref, z_ref, alog_ref, d_ref, ng_ref, o_ref, state_ref, y_ref):
    @pl.when(pl.program_id(1) == 0)
    def _():
        state_ref[...] = jnp.zeros_like(state_ref)

    L = xs_ref.shape[0]
    row = _iota2((L, L), 0)
    col = _iota2((L, L), 1)
    lower = row >= col
    tril = lower.astype(F32)
    eye = (_iota2((LANE, LANE), 0) == _iota2((LANE, LANE), 1)).astype(F32)

    dt = dt_ref[...]
    da = dt * (-jnp.exp(alog_ref[...]))
    acs = _dot_exact(tril, da)
    acs_t = _dot_nt_exact(eye, acs)
    tot = acs[L - 1:L, :]
    decay_out = jnp.exp(acs)
    decay_end = jnp.exp(tot - acs)
    decay_tot = jnp.exp(tot)

    for g in range(SSM_GROUPS):
        gs = slice(SSM_STATE * g, SSM_STATE * (g + 1))
        bm = b_ref[:, gs]
        cm = c_ref[:, gs]
        cb = _dot_nt(cm, bm)
        bm_t = _dot_nt(eye, bm)
        for e in range(SSM_HPG):
            hd = SSM_HPG * g + e
            hs = slice(SSM_HEAD * hd, SSM_HEAD * (hd + 1))
            xs = xs_ref[:, hs]
            xdt = xs * dt[:, hd:hd + 1]
            seg = jnp.where(lower, jnp.exp(acs[:, hd:hd + 1] - acs_t[hd:hd + 1, :]), 0.0)
            st = state_ref[hd]
            y = _dot(cb * seg, xdt) + _dot(cm, st) * decay_out[:, hd:hd + 1]
            state_ref[hd] = st * decay_tot[:, hd:hd + 1] + _dot(bm_t, xdt * decay_end[:, hd:hd + 1])
            y_ref[:, hs] = y + xs * d_ref[:, hd:hd + 1]

    z = z_ref[...]
    y = y_ref[...] * (z * _sigmoid(z))
    gw = SSM_DIM // SSM_GROUPS
    for g in range(SSM_GROUPS):
        gs = slice(gw * g, gw * (g + 1))
        yg = y[:, gs]
        yg = yg * lax.rsqrt(jnp.mean(yg * yg, axis=-1, keepdims=True) + SSM_NORM_EPS)
        o_ref[:, gs] = (yg * ng_ref[:, gs]).astype(BF16)


def _ssm_branch(x2, bsz, seq, gmix, wp):
    n_tok = x2.shape[0]
    tm = min(TOK_TILE, seq)
    ns = seq // tm
    gn = SSM_GROUPS * SSM_STATE
    tok = lambda w: pl.BlockSpec((tm, w), lambda b, i: (b * ns + i, 0))
    z, xs, bm, cm, dt = pl.pallas_call(
        _ssm_prep_kernel,
        grid=(bsz, ns),
        in_specs=[tok(D_MODEL), _full((1, D_MODEL)), _full(wp["w_ssm"].shape), _full((SSM_CONV, SSM_CONV_DIM)),
                  _full((1, SSM_CONV_DIM)), _full((1, LANE))],
        out_specs=[tok(SSM_DIM), tok(SSM_DIM), tok(gn), tok(gn), tok(LANE)],
        out_shape=[jax.ShapeDtypeStruct((n_tok, SSM_DIM), F32), jax.ShapeDtypeStruct((n_tok, SSM_DIM), F32),
                   jax.ShapeDtypeStruct((n_tok, gn), BF16), jax.ShapeDtypeStruct((n_tok, gn), BF16),
                   jax.ShapeDtypeStruct((n_tok, LANE), F32)],
        scratch_shapes=[pltpu.VMEM((SUBLANE, SSM_CONV_DIM), F32)],
        compiler_params=_params("parallel", "arbitrary"),
        name="ssm_prep",
    )(x2, gmix, wp["w_ssm"], wp["conv_w"], wp["conv_b"], wp["dt_bias"])

    assert seq % SSM_CHUNK == 0
    nc = seq // SSM_CHUNK
    ch = lambda w: pl.BlockSpec((SSM_CHUNK, w), lambda b, c: (b * nc + c, 0))
    return pl.pallas_call(
        _ssd_kernel,
        grid=(bsz, nc),
        in_specs=[ch(SSM_DIM), ch(gn), ch(gn), ch(LANE), ch(SSM_DIM), _full((1, LANE)), _full((1, LANE)),
                  _full((1, SSM_DIM))],
        out_specs=ch(SSM_DIM),
        out_shape=jax.ShapeDtypeStruct((n_tok, SSM_DIM), BF16),
        scratch_shapes=[pltpu.VMEM((SSM_HEADS, SSM_STATE, SSM_HEAD), F32), pltpu.VMEM((SSM_CHUNK, SSM_DIM), F32)],
        compiler_params=_params("parallel", "arbitrary"),
        name="ssd",
    )(xs, bm, cm, dt, z, wp["a_log"], wp["d_skip"], wp["norm_g"])


def _merge_kernel(x_ref, gmix_ref, wg_ref, om_ref, or_ref, os_ref, wm_ref, wr_ref, ws_ref, wo_ref, o_ref):
    x = x_ref[...]
    h = _rms(x, gmix_ref[...], NORM_EPS).astype(BF16)
    d = D_MODEL
    merged = _sigmoid(_dot(h, wg_ref[:, :d])) * _dot(om_ref[...], wm_ref[...])
    merged = merged + _sigmoid(_dot(h, wg_ref[:, d:2 * d])) * _dot(or_ref[...], wr_ref[...])
    merged = merged + _sigmoid(_dot(h, wg_ref[:, 2 * d:])) * _dot(os_ref[...], ws_ref[...])
    o_ref[...] = x + _dot(merged, wo_ref[...])


def _ffn_kernel(x_ref, g_ref, w1_ref, w2_ref, o_ref, h_ref):
    j = pl.program_id(1)

    @pl.when(j == 0)
    def _():
        x = x_ref[...]
        h_ref[...] = _rms(x, g_ref[...], NORM_EPS).astype(BF16)
        o_ref[...] = x

    u = jnp.maximum(_dot(h_ref[...], w1_ref[...]), 0.0)
    o_ref[...] += _dot(u * u, w2_ref[...])


def _merge_ffn(x2, gmix, wp, o_mla, o_rwkv, o_ssm):
    n_tok = x2.shape[0]
    tm = min(TOK_TILE, n_tok)
    tok = lambda w: pl.BlockSpec((tm, w), lambda i: (i, 0))
    x2 = pl.pallas_call(
        _merge_kernel,
        grid=(n_tok // tm,),
        in_specs=[tok(D_MODEL), _full((1, D_MODEL)), _full((D_MODEL, GATE_IN)), tok(o_mla.shape[1]),
                  tok(o_rwkv.shape[1]), tok(o_ssm.shape[1]), _full(wp["w_br_mla"].shape),
                  _full(wp["w_br_rwkv"].shape), _full(wp["w_br_ssm"].shape), _full((D_MODEL, D_MODEL))],
        out_specs=tok(D_MODEL),
        out_shape=jax.ShapeDtypeStruct((n_tok, D_MODEL), F32),
        compiler_params=_params("parallel"),
        name="merge",
    )(x2, gmix, wp["w_gate"], o_mla, o_rwkv, o_ssm, wp["w_br_mla"], wp["w_br_rwkv"], wp["w_br_ssm"], wp["w_out"])

    tf = min(FFN_TOK_TILE, n_tok)
    ff = FFN_FF_TILE
    return pl.pallas_call(
        _ffn_kernel,
        grid=(n_tok // tf, D_FF // ff),
        in_specs=[pl.BlockSpec((tf, D_MODEL), lambda i, j: (i, 0)), _full((1, D_MODEL)),
                  pl.BlockSpec((D_MODEL, ff), lambda i, j: (0, j)), pl.BlockSpec((ff, D_MODEL), lambda i, j: (j, 0))],
        out_specs=pl.BlockSpec((tf, D_MODEL), lambda i, j: (i, 0)),
        out_shape=jax.ShapeDtypeStruct((n_tok, D_MODEL), F32),
        scratch_shapes=[pltpu.VMEM((tf, D_MODEL), BF16)],
        compiler_params=_params("parallel", "arbitrary"),
        name="ffn",
    )(x2, wp["norm_ffn_g"], wp["w_ff1"], wp["w_ff2"])


def _pad_cols(w, width):
    return jnp.pad(w, ((0, 0), (0, width - w.shape[1])))


def _pad_rows(w, height, before=0):
    return jnp.pad(w, ((before, height - before - w.shape[0]), (0, 0)))


def _row(v, width=None):
    v = v.reshape(1, -1).astype(F32)
    return v if width is None else _pad_cols(v, width)


def _layer_weights(l, p):
    w_in = p["w_in"][l]
    o_r = MLA_IN
    o_s = o_r + RWKV_IN
    o_g = o_s + SSM_IN
    zc = lambda n: jnp.zeros((D_MODEL, n), w_in.dtype)
    w_mla = jnp.concatenate([w_in[:, :MLA_Q_RANK + MLA_KV_RANK], zc(MLA_NOPE), w_in[:, MLA_Q_RANK + MLA_KV_RANK:MLA_IN],
                             zc(LANE - MLA_QK)], axis=1)
    w_uq = p["mla_w_uq"][l].reshape(MLA_Q_RANK, MLA_HEADS, MLA_QK)
    w_uq = jnp.pad(w_uq, ((0, 0), (0, 0), (0, LANE - MLA_QK))).reshape(MLA_Q_RANK, MLA_HEADS * LANE)
    w_ukv = p["mla_w_ukv"][l].reshape(MLA_KV_RANK, MLA_HEADS, MLA_NOPE + MLA_V)
    w_uk = jnp.pad(w_ukv[:, :, :MLA_NOPE], ((0, 0), (0, 0), (0, LANE - MLA_NOPE))).reshape(MLA_KV_RANK, MLA_HEADS * LANE)
    w_uv = w_ukv[:, :, MLA_NOPE:].reshape(MLA_KV_RANK, MLA_HEADS * MLA_V)
    w_ssm = _pad_cols(w_in[:, o_s:o_g], SSM_DIM + SSM_CONV_DIM + LANE)
    wp = {
        "w_mla": w_mla.astype(BF16), "w_uq": w_uq.astype(BF16), "w_uk": w_uk.astype(BF16), "w_uv": w_uv.astype(BF16),
        "q_norm_g": _row(p["mla_q_norm_g"][l]), "kv_norm_g": _row(p["mla_kv_norm_g"][l]),
        "q_head_g": _row(p["mla_q_head_g"][l], LANE), "k_head_g": _row(p["mla_k_head_g"][l], LANE),
        "w_rwkv": w_in[:, o_r:o_s].astype(BF16), "mu": _row(p["rwkv_mu"][l]), "w0": _row(p["rwkv_w0"][l]),
        "w2": _pad_rows(p["rwkv_w2"][l], LANE).astype(BF16), "a0": _row(p["rwkv_a0"][l]),
        "a2": _pad_rows(p["rwkv_a2"][l], LANE, before=RWKV_W_RANK).astype(BF16), "g2": p["rwkv_g2"][l].astype(BF16),
        "k_k": _row(p["rwkv_k_k"][l]), "k_a": _row(p["rwkv_k_a"][l]), "r_k": _row(p["rwkv_r_k"][l]),
        "ln_g": _row(p["rwkv_ln_g"][l]), "ln_b": _row(p["rwkv_ln_b"][l]),
        "w_ssm": w_ssm.astype(BF16), "conv_w": p["ssm_conv_w"][l].astype(F32), "conv_b": _row(p["ssm_conv_b"][l]),
        "dt_bias": _row(p["ssm_dt_bias"][l], LANE), "a_log": _row(p["ssm_a_log"][l], LANE),
        "d_skip": _row(p["ssm_d"][l], LANE), "norm_g": _row(p["ssm_norm_g"][l]),
        "w_gate": w_in[:, o_g:].astype(BF16), "w_br_mla": p["w_br_mla"][l].astype(BF16),
        "w_br_rwkv": p["w_br_rwkv"][l].astype(BF16), "w_br_ssm": p["w_br_ssm"][l].astype(BF16),
        "w_out": p["w_out"][l].astype(BF16), "norm_ffn_g": _row(p["norm_ffn_g"][l]),
        "w_ff1": p["w_ff1"][l].astype(BF16), "w_ff2": p["w_ff2"][l].astype(BF16),
    }
    if l > 0:
        wp["v0"] = _row(p["rwkv_v0"][l - 1])
        wp["v1"] = _pad_cols(p["rwkv_v1"][l - 1], LANE).astype(BF16)
        wp["v2"] = _pad_rows(p["rwkv_v2"][l - 1], LANE).astype(BF16)
    return wp


def kernel(x, positions, norm_mix_g, w_in, mla_q_norm_g, mla_kv_norm_g, mla_w_uq, mla_w_ukv, mla_q_head_g, mla_k_head_g, rwkv_mu, rwkv_w0, rwkv_w2, rwkv_a0, rwkv_a2, rwkv_g2, rwkv_v0, rwkv_v1, rwkv_v2, rwkv_k_k, rwkv_k_a, rwkv_r_k, rwkv_ln_g, rwkv_ln_b, ssm_conv_w, ssm_conv_b, ssm_dt_bias, ssm_a_log, ssm_d, ssm_norm_g, w_br_mla, w_br_rwkv, w_br_ssm, w_out, norm_ffn_g, w_ff1, w_ff2):
    p = dict(w_in=w_in, mla_q_norm_g=mla_q_norm_g, mla_kv_norm_g=mla_kv_norm_g, mla_w_uq=mla_w_uq,
             mla_w_ukv=mla_w_ukv, mla_q_head_g=mla_q_head_g, mla_k_head_g=mla_k_head_g, rwkv_mu=rwkv_mu,
             rwkv_w0=rwkv_w0, rwkv_w2=rwkv_w2, rwkv_a0=rwkv_a0, rwkv_a2=rwkv_a2, rwkv_g2=rwkv_g2, rwkv_v0=rwkv_v0,
             rwkv_v1=rwkv_v1, rwkv_v2=rwkv_v2, rwkv_k_k=rwkv_k_k, rwkv_k_a=rwkv_k_a, rwkv_r_k=rwkv_r_k,
             rwkv_ln_g=rwkv_ln_g, rwkv_ln_b=rwkv_ln_b, ssm_conv_w=ssm_conv_w, ssm_conv_b=ssm_conv_b,
             ssm_dt_bias=ssm_dt_bias, ssm_a_log=ssm_a_log, ssm_d=ssm_d, ssm_norm_g=ssm_norm_g, w_br_mla=w_br_mla,
             w_br_rwkv=w_br_rwkv, w_br_ssm=w_br_ssm, w_out=w_out, norm_ffn_g=norm_ffn_g, w_ff1=w_ff1, w_ff2=w_ff2)
    bsz, seq, d_model = x.shape
    depth = w_in.shape[0]
    x2 = x.reshape(bsz * seq, d_model)
    tables = _rope_tables(positions)
    v_first = None
    for l in range(depth):
        wp = _layer_weights(l, p)
        gmix = _row(norm_mix_g[l])
        o_mla = _mla_branch(x2, bsz, seq, gmix, wp, tables)
        o_rwkv, v = _rwkv_branch(x2, bsz, seq, gmix, wp, v_first)
        if l == 0:
            v_first = v
        o_ssm = _ssm_branch(x2, bsz, seq, gmix, wp)
        x2 = _merge_ffn(x2, gmix, wp, o_mla, o_rwkv, o_ssm)
    return x2.reshape(bsz, seq, d_model)
```

```python
import functools

import jax
import jax.numpy as jnp
from jax import lax
from jax.experimental import pallas as pl
from jax.experimental.pallas import tpu as pltpu

F32 = jnp.float32
BF16 = jnp.bfloat16

D_MODEL = 1024
MLA_HEADS = 8
MLA_NOPE = 64
MLA_ROPE = 32
MLA_QK = MLA_NOPE + MLA_ROPE
MLA_V = 64
MLA_Q_RANK = 384
MLA_KV_RANK = 256
ROPE_THETA = 10000.0
RWKV_HEAD = 64
RWKV_HEADS = 8
RWKV_DIM = RWKV_HEADS * RWKV_HEAD
RWKV_W_RANK = 64
RWKV_A_RANK = 64
RWKV_V_RANK = 32
RWKV_G_RANK = 128
RWKV_GN_EPS = 64e-5
SSM_HEAD = 64
SSM_HEADS = 16
SSM_DIM = SSM_HEADS * SSM_HEAD
SSM_GROUPS = 2
SSM_HPG = SSM_HEADS // SSM_GROUPS
SSM_STATE = 128
SSM_CONV = 4
SSM_CHUNK = 256
SSM_CONV_DIM = SSM_DIM + 2 * SSM_GROUPS * SSM_STATE
SSM_NORM_EPS = 1e-5
N_BRANCH = 3
D_FF = 4 * D_MODEL
NORM_EPS = 1e-6
MLA_IN = MLA_Q_RANK + MLA_KV_RANK + MLA_ROPE
RWKV_IN = 3 * RWKV_DIM + RWKV_W_RANK + RWKV_A_RANK + RWKV_G_RANK
SSM_IN = SSM_DIM + SSM_CONV_DIM + SSM_HEADS
GATE_IN = N_BRANCH * D_MODEL

LOG2_E = 1.4426950408889634
LANE = 128
SUBLANE = 8
VMEM_LIMIT = 56 * 1024 * 1024

TOK_TILE = 512
FFN_TOK_TILE = 1024
FFN_FF_TILE = 1024
ATTN_TILE = 512
ATTN_HEADS = 8
ATTN_KEYS = 256
RWKV_CHUNK = 64
RWKV_TILE = 512
RWKV_SCAN_HEADS = 4


def _dot(a, b):
    return jnp.dot(a.astype(BF16), b.astype(BF16), preferred_element_type=F32)


def _dot_nt(a, b):
    return lax.dot_general(a.astype(BF16), b.astype(BF16), (((1,), (1,)), ((), ())),
                           preferred_element_type=F32)


def _dot_tn(a, b):
    return lax.dot_general(a.astype(BF16), b.astype(BF16), (((0,), (0,)), ((), ())),
                           preferred_element_type=F32)


def _bmm(a, b):
    return lax.dot_general(a.astype(BF16), b.astype(BF16), (((2,), (1,)), ((0,), (0,))),
                           preferred_element_type=F32)


def _bmm_nt(a, b):
    return lax.dot_general(a.astype(BF16), b.astype(BF16), (((2,), (2,)), ((0,), (0,))),
                           preferred_element_type=F32)


def _bmm_tn(a, b):
    return lax.dot_general(a.astype(BF16), b.astype(BF16), (((1,), (1,)), ((0,), (0,))),
                           preferred_element_type=F32)


def _dot_exact(a, b):
    return jnp.dot(a, b, preferred_element_type=F32, precision=lax.Precision.HIGHEST)


def _dot_nt_exact(a, b):
    return lax.dot_general(a, b, (((1,), (1,)), ((), ())), preferred_element_type=F32,
                           precision=lax.Precision.HIGHEST)


def _rms(x, g, eps):
    return x * lax.rsqrt(jnp.mean(x * x, axis=-1, keepdims=True) + eps) * g


def _sigmoid(x):
    return 1.0 / (1.0 + jnp.exp(-x))


def _softplus(x):
    return jnp.maximum(x, 0.0) + jnp.log(1.0 + jnp.exp(-jnp.abs(x)))


def _iota2(shape, dim):
    return lax.broadcasted_iota(jnp.int32, shape, dim)


def _params(*sem):
    return pltpu.CompilerParams(dimension_semantics=sem, vmem_limit_bytes=VMEM_LIMIT)


def _full(shape):
    return pl.BlockSpec(shape, lambda *_: (0,) * len(shape))


def _rope_table_kernel(pos_ref, freq_ref, cos_ref, sin_ref):
    ang = pos_ref[...].astype(F32) * freq_ref[...]
    cos_ref[...] = jnp.cos(ang)
    sin_ref[...] = jnp.sin(ang)


def _rope_tables(positions):
    half = MLA_ROPE // 2
    n_tok = positions.size
    inv_freq = ROPE_THETA ** (-jnp.arange(half, dtype=jnp.float32) / half)
    per_row = LANE // half
    rows = n_tok // per_row
    pos = jnp.repeat(positions.reshape(rows, per_row), half, axis=1)
    freq = jnp.tile(inv_freq, per_row).reshape(1, LANE)
    blk = min(rows, 512)
    cos, sin = pl.pallas_call(
        _rope_table_kernel,
        grid=(rows // blk,),
        in_specs=[pl.BlockSpec((blk, LANE), lambda i: (i, 0)), _full((1, LANE))],
        out_specs=[pl.BlockSpec((blk, LANE), lambda i: (i, 0))] * 2,
        out_shape=[jax.ShapeDtypeStruct((rows, LANE), F32)] * 2,
        compiler_params=_params("parallel"),
        name="rope_table",
    )(pos, freq)
    cos = cos.reshape(n_tok, half)
    sin = sin.reshape(n_tok, half)
    one = jnp.ones((n_tok, MLA_NOPE), F32)
    z_nope = jnp.zeros((n_tok, MLA_NOPE), F32)
    z_half = jnp.zeros((n_tok, half), F32)
    z_pad = jnp.zeros((n_tok, LANE - MLA_QK), F32)
    cosf = jnp.concatenate([one, cos, cos, z_pad], axis=1)
    sina = jnp.concatenate([z_nope, -sin, z_half, z_pad], axis=1)
    sinb = jnp.concatenate([z_nope, z_half, sin, z_pad], axis=1)
    return cosf, sina, sinb


def _mla_prep_kernel(x_ref, gmix_ref, wmla_ref, qg_ref, kvg_ref, wuq_ref, wuk_ref, wuv_ref,
                     qhg_ref, khg_ref, cos_ref, sina_ref, sinb_ref, q_ref, k_ref, v_ref):
    h = _rms(x_ref[...], gmix_ref[...], NORM_EPS)
    c = _dot(h, wmla_ref[...])
    cq = _rms(c[:, :MLA_Q_RANK], qg_ref[...], NORM_EPS)
    ckv = _rms(c[:, MLA_Q_RANK:MLA_Q_RANK + MLA_KV_RANK], kvg_ref[...], NORM_EPS)
    krope = c[:, MLA_Q_RANK + MLA_KV_RANK:]
    qf = _dot(cq, wuq_ref[...])
    kf = _dot(ckv, wuk_ref[...])
    v_ref[...] = _dot(ckv, wuv_ref[...]).astype(BF16)
    cosf, sina, sinb = cos_ref[...], sina_ref[...], sinb_ref[...]
    qhg, khg = qhg_ref[...], khg_ref[...]
    scale = MLA_QK ** -0.5 * LOG2_E

    def norm_rope(t, g):
        t = t * lax.rsqrt(jnp.sum(t * t, axis=-1, keepdims=True) * (1.0 / MLA_QK) + NORM_EPS) * g
        up = pltpu.roll(t, LANE - MLA_ROPE // 2, 1)
        down = pltpu.roll(t, MLA_ROPE // 2, 1)
        return t * cosf + up * sina + down * sinb

    for hd in range(MLA_HEADS):
        sl = slice(LANE * hd, LANE * (hd + 1))
        q_ref[:, sl] = (norm_rope(qf[:, sl], qhg) * scale).astype(BF16)
        k_ref[:, sl] = norm_rope(kf[:, sl] + krope, khg).astype(BF16)


def _attn_kernel(q_ref, k_ref, v_ref, o_ref, m_ref, l_ref, acc_ref):
    i = pl.program_id(2)
    tq = q_ref.shape[0]
    heads = range(ATTN_HEADS)
    m_ref[...] = jnp.full(m_ref.shape, -jnp.inf, F32)
    l_ref[...] = jnp.zeros(l_ref.shape, F32)
    acc_ref[...] = jnp.zeros(acc_ref.shape, F32)

    def step(j, masked):
        for sub in range(tq // ATTN_KEYS):
            substep(j * tq + sub * ATTN_KEYS, sub * ATTN_KEYS, masked)

    def substep(start, offset, masked):
        rows = pl.ds(pl.multiple_of(start, ATTN_KEYS), ATTN_KEYS)
        scores = [_dot_nt(k_ref[rows, LANE * hh:LANE * (hh + 1)], q_ref[:, LANE * hh:LANE * (hh + 1)])
                  for hh in heads]
        probs = []
        for hh in heads:
            s = scores[hh]
            if masked:
                s = jnp.where(_iota2(s.shape, 1) >= _iota2(s.shape, 0) + offset, s, -jnp.inf)
            m = m_ref[hh]
            m_new = jnp.maximum(m, jnp.max(s, axis=0, keepdims=True))
            p = jnp.exp2(s - m_new)
            alpha = jnp.exp2(m - m_new)
            m_ref[hh] = m_new
            l_ref[hh] = alpha * l_ref[hh] + jnp.sum(p, axis=0, keepdims=True)
            probs.append((p.astype(BF16), alpha))
        for hh in heads:
            p, alpha = probs[hh]
            v = v_ref[rows, MLA_V * hh:MLA_V * (hh + 1)]
            acc_ref[hh] = alpha * acc_ref[hh] + _dot_tn(v, p)

    def body(j, carry):
        step(j, False)
        return carry

    lax.fori_loop(0, i, body, 0)
    step(i, True)
    out = jnp.concatenate([acc_ref[hh] / l_ref[hh] for hh in heads], axis=0)
    o_ref[...] = out.T.astype(BF16)


def _mla_branch(x2, bsz, seq, gmix, wp, tables):
    n_tok = x2.shape[0]
    tm = min(TOK_TILE, seq)
    cosf, sina, sinb = tables
    tok = lambda w: pl.BlockSpec((tm, w), lambda i: (i, 0))
    hq = MLA_HEADS * LANE
    q, k, v = pl.pallas_call(
        _mla_prep_kernel,
        grid=(n_tok // tm,),
        in_specs=[tok(D_MODEL), _full((1, D_MODEL)), _full(wp["w_mla"].shape), _full((1, MLA_Q_RANK)),
                  _full((1, MLA_KV_RANK)), _full(wp["w_uq"].shape), _full(wp["w_uk"].shape),
                  _full(wp["w_uv"].shape), _full((1, LANE)), _full((1, LANE)), tok(LANE), tok(LANE), tok(LANE)],
        out_specs=[tok(hq), tok(hq), tok(MLA_HEADS * MLA_V)],
        out_shape=[jax.ShapeDtypeStruct((n_tok, hq), BF16), jax.ShapeDtypeStruct((n_tok, hq), BF16),
                   jax.ShapeDtypeStruct((n_tok, MLA_HEADS * MLA_V), BF16)],
        compiler_params=_params("parallel"),
        name="mla_prep",
    )(x2, gmix, wp["w_mla"], wp["q_norm_g"], wp["kv_norm_g"], wp["w_uq"], wp["w_uk"], wp["w_uv"],
      wp["q_head_g"], wp["k_head_g"], cosf, sina, sinb)

    tq = min(ATTN_TILE, seq)
    nq = seq // tq
    return pl.pallas_call(
        _attn_kernel,
        grid=(bsz, MLA_HEADS // ATTN_HEADS, nq),
        in_specs=[pl.BlockSpec((tq, ATTN_HEADS * LANE), lambda b, hp, i: (b * nq + i, hp)),
                  pl.BlockSpec((seq, ATTN_HEADS * LANE), lambda b, hp, i: (b, hp)),
                  pl.BlockSpec((seq, ATTN_HEADS * MLA_V), lambda b, hp, i: (b, hp))],
        out_specs=pl.BlockSpec((tq, ATTN_HEADS * MLA_V), lambda b, hp, i: (b * nq + i, hp)),
        out_shape=jax.ShapeDtypeStruct((n_tok, MLA_HEADS * MLA_V), BF16),
        scratch_shapes=[pltpu.VMEM((ATTN_HEADS, 1, tq), F32), pltpu.VMEM((ATTN_HEADS, 1, tq), F32),
                        pltpu.VMEM((ATTN_HEADS, MLA_V, tq), F32)],
        compiler_params=_params("parallel", "parallel", "arbitrary"),
        name="mla_attention",
    )(q, k, v)


def _rwkv_prep_kernel(has_vres, *refs):
    if has_vres:
        (x_ref, gmix_ref, w_ref, mu_ref, w0_ref, w2_ref, a0_ref, a2_ref, g2_ref, kk_ref, ka_ref,
         vfirst_ref, v0_ref, v1_ref, v2_ref,
         r_out, lw_out, k_out, v_out, kk_out, a_out, g_out, carry_ref) = refs
    else:
        (x_ref, gmix_ref, w_ref, mu_ref, w0_ref, w2_ref, a0_ref, a2_ref, g2_ref, kk_ref, ka_ref,
         r_out, lw_out, k_out, v_out, kk_out, a_out, g_out, carry_ref) = refs

    @pl.when(pl.program_id(1) == 0)
    def _():
        carry_ref[...] = jnp.zeros_like(carry_ref)

    h = _rms(x_ref[...], gmix_ref[...], NORM_EPS)
    p = _dot(h, w_ref[...])
    tm = p.shape[0]
    rolled = pltpu.roll(p, 1, 0)
    prev = jnp.where(_iota2(p.shape, 0) == 0, carry_ref[...], rolled)
    carry_ref[...] = p[tm - 1:tm, :]
    p = p + (prev - p) * mu_ref[...]

    d = RWKV_DIM
    r, k, v = p[:, :d], p[:, d:2 * d], p[:, 2 * d:3 * d]
    xwa = p[:, 3 * d:3 * d + LANE]
    xg = p[:, 3 * d + LANE:]
    w = -_softplus(-(w0_ref[...] + _dot(jnp.tanh(xwa), w2_ref[...]))) - 0.5
    lw_out[...] = -jnp.exp(w)
    if has_vres:
        gate = _sigmoid(v0_ref[...] + _dot(_dot(v, v1_ref[...]), v2_ref[...]))
        v = v + (vfirst_ref[...] - v) * gate
    a = _sigmoid(a0_ref[...] + _dot(xwa, a2_ref[...]))
    g_out[...] = _dot(_sigmoid(xg), g2_ref[...])
    r_out[...] = r
    kk_out[...] = k * kk_ref[...]
    k_out[...] = k * (1.0 + (a - 1.0) * ka_ref[...])
    v_out[...] = v
    a_out[...] = a


def _head_sum(x):
    blocks = []
    for j in range(x.shape[-1] // LANE):
        xb = x[..., LANE * j:LANE * (j + 1)]
        first = _iota2(xb.shape, xb.ndim - 1) < RWKV_HEAD
        s0 = jnp.sum(jnp.where(first, xb, 0.0), axis=-1, keepdims=True)
        s1 = jnp.sum(jnp.where(first, 0.0, xb), axis=-1, keepdims=True)
        blocks.append(jnp.where(first, s0, s1))
    return blocks[0] if len(blocks) == 1 else jnp.concatenate(blocks, axis=-1)


def _rwkv_scan_kernel(r_ref, lw_ref, k_ref, v_ref, kk_ref, a_ref, g_ref, rk_ref, lng_ref, lnb_ref,
                      o_ref, state_ref, y_ref):
    L = RWKV_CHUNK
    n = RWKV_HEAD
    ts, width = r_ref.shape
    nh = width // n
    nchunk = ts // L
    sh3 = (nchunk, L, width)

    @pl.when(pl.program_id(2) == 0)
    def _():
        state_ref[...] = jnp.zeros_like(state_ref)

    t_idx = _iota2(sh3, 1)
    r = r_ref[...].reshape(sh3)
    lw = lw_ref[...].reshape(sh3)
    k = k_ref[...].reshape(sh3)
    v = v_ref[...].reshape(sh3)
    kk = kk_ref[...].reshape(sh3)
    a = a_ref[...].reshape(sh3)

    kk = kk / jnp.maximum(jnp.sqrt(_head_sum(kk * kk)), 1e-12)
    cum = lw
    s = 1
    while s < L:
        cum = cum + jnp.where(t_idx >= s, pltpu.roll(cum, s, 1), 0.0)
        s *= 2
    tot = cum[:, L - 1:L, :]
    inv = jnp.exp(-cum)
    to_end = jnp.exp(tot - cum)

    def heads_to_batch(x):
        return jnp.concatenate([x[:, :, n * hh:n * (hh + 1)] for hh in range(nh)], axis=0)

    at = heads_to_batch(-kk * jnp.exp(cum - lw))
    rt = heads_to_batch(r * jnp.exp(cum))
    bt = heads_to_batch(kk * a * inv)
    kt = heads_to_batch(k * inv)
    bend = heads_to_batch(kk * a * to_end).astype(BF16)
    kend = heads_to_batch(k * to_end)
    vb = heads_to_batch(v)
    tot2 = tot.reshape(nchunk, width)
    pad = jnp.zeros((LANE - nchunk, LANE), F32)
    gam_t = [jnp.exp(jnp.concatenate([tot2[:, LANE * j:LANE * (j + 1)], pad], axis=0).T)
             for j in range(width // LANE)]

    nb = nh * nchunk
    row = _iota2((nb, 2 * L, L), 1)
    col = _iota2((nb, 2 * L, L), 2)
    causal = row - col >= jnp.where(row < L, 1, L)
    eye = (_iota2((nb, L, L), 1) == _iota2((nb, L, L), 2)).astype(F32)

    ar = jnp.concatenate([at, rt], axis=1)
    pb = jnp.where(causal, _bmm_nt(ar, bt), 0.0)
    pk = jnp.where(causal, _bmm_nt(ar, kt), 0.0)
    wv = _bmm(pk, vb)
    mab = pb[:, :L]
    tinv = eye + mab
    pw = _bmm(mab, mab)
    s = 4
    while s < L:
        both = _bmm(jnp.concatenate([tinv, pw], axis=1), pw)
        tinv = tinv + both[:, :L]
        pw = both[:, L:]
        s *= 2
    tinv = tinv + _bmm(tinv, pw)
    abar = _bmm(tinv, at)
    u0 = _bmm(tinv, wv[:, :L])
    wmat = _bmm_tn(bend, abar).astype(BF16)
    nmat = _bmm_tn(bend, u0) + _bmm_tn(kend, vb)
    ar = jnp.concatenate([abar, rt], axis=1).astype(BF16)
    nrb = pb[:, L:].astype(BF16)
    y0 = wv[:, L:]

    hs = range(nh)
    st = [state_ref[hh] for hh in hs]
    for c in range(nchunk):
        idx = [hh * nchunk + c for hh in hs]
        upd = [_dot(wmat[idx[hh]], st[hh]) for hh in hs]
        rd = [_dot(ar[idx[hh]], st[hh]) for hh in hs]
        for hh in hs:
            u = rd[hh][:L] + u0[idx[hh]]
            y_ref[L * c:L * (c + 1), n * hh:n * (hh + 1)] = rd[hh][L:] + _dot(nrb[idx[hh]], u) + y0[idx[hh]]
            gcol = gam_t[hh // 2][n * (hh % 2):n * (hh % 2 + 1), c:c + 1]
            st[hh] = st[hh] * gcol + upd[hh] + nmat[idx[hh]]
    for hh in hs:
        state_ref[hh] = st[hh]

    y = y_ref[...]
    mean = _head_sum(y) * (1.0 / n)
    var = _head_sum(jnp.square(y - mean)) * (1.0 / n)
    y = (y - mean) * lax.rsqrt(var + RWKV_GN_EPS)
    y = y * lng_ref[...] + lnb_ref[...]
    y = y + _head_sum(r_ref[...] * k_ref[...] * rk_ref[...]) * v_ref[...]
    o_ref[...] = (y * g_ref[...]).astype(BF16)


def _rwkv_branch(x2, bsz, seq, gmix, wp, v_first):
    n_tok = x2.shape[0]
    tm = min(TOK_TILE, seq)
    ns = seq // tm
    has_vres = v_first is not None
    d = RWKV_DIM
    tok = lambda w: pl.BlockSpec((tm, w), lambda b, i: (b * ns + i, 0))
    vec = _full((1, d))
    in_specs = [tok(D_MODEL), _full((1, D_MODEL)), _full((D_MODEL, RWKV_IN)), _full((1, RWKV_IN)), vec,
                _full((LANE, d)), vec, _full((LANE, d)), _full((RWKV_G_RANK, d)), vec, vec]
    args = [x2, gmix, wp["w_rwkv"], wp["mu"], wp["w0"], wp["w2"], wp["a0"], wp["a2"], wp["g2"], wp["k_k"], wp["k_a"]]
    if has_vres:
        in_specs += [tok(d), vec, _full((d, LANE)), _full((LANE, d))]
        args += [v_first, wp["v0"], wp["v1"], wp["v2"]]
    r, lw, k, v, kk, a, g = pl.pallas_call(
        functools.partial(_rwkv_prep_kernel, has_vres),
        grid=(bsz, ns),
        in_specs=in_specs,
        out_specs=[tok(d)] * 7,
        out_shape=[jax.ShapeDtypeStruct((n_tok, d), F32)] * 7,
        scratch_shapes=[pltpu.VMEM((1, RWKV_IN), F32)],
        compiler_params=_params("parallel", "arbitrary"),
        name="rwkv_prep",
    )(*args)

    ts = min(RWKV_TILE, seq)
    nt = seq // ts
    width = RWKV_SCAN_HEADS * RWKV_HEAD
    blk = pl.BlockSpec((ts, width), lambda b, hp, i: (b * nt + i, hp))
    par = pl.BlockSpec((1, width), lambda b, hp, i: (0, hp))
    o = pl.pallas_call(
        _rwkv_scan_kernel,
        grid=(bsz, RWKV_HEADS // RWKV_SCAN_HEADS, nt),
        in_specs=[blk] * 7 + [par] * 3,
        out_specs=blk,
        out_shape=jax.ShapeDtypeStruct((n_tok, d), BF16),
        scratch_shapes=[pltpu.VMEM((RWKV_SCAN_HEADS, RWKV_HEAD, RWKV_HEAD), F32), pltpu.VMEM((ts, width), F32)],
        compiler_params=_params("parallel", "parallel", "arbitrary"),
        name="rwkv_scan",
    )(r, lw, k, v, kk, a, g, wp["r_k"], wp["ln_g"], wp["ln_b"])
    return o, v


def _ssm_prep_kernel(x_ref, gmix_ref, w_ref, cw_ref, cb_ref, dtb_ref,
                     z_out, xs_out, b_out, c_out, dt_out, carry_ref):
    @pl.when(pl.program_id(1) == 0)
    def _():
        carry_ref[...] = jnp.zeros_like(carry_ref)

    h = _rms(x_ref[...], gmix_ref[...], NORM_EPS)
    p = _dot(h, w_ref[...])
    z_out[...] = p[:, :SSM_DIM]
    xbc = p[:, SSM_DIM:SSM_DIM + SSM_CONV_DIM]
    dt_out[...] = _softplus(p[:, SSM_DIM + SSM_CONV_DIM:] + dtb_ref[...])

    tm = xbc.shape[0]
    prev = carry_ref[...]
    carry_ref[...] = xbc[tm - SUBLANE:, :]
    top_row = _iota2((SUBLANE, SSM_CONV_DIM), 0)
    acc = xbc * cw_ref[SSM_CONV - 1:SSM_CONV, :] + cb_ref[...]
    for s in range(1, SSM_CONV):
        rolled = pltpu.roll(xbc, s, 0)
        top = jnp.where(top_row < s, pltpu.roll(prev, s, 0), rolled[:SUBLANE])
        shifted = jnp.concatenate([top, rolled[SUBLANE:]], axis=0)
        acc = acc + shifted * cw_ref[SSM_CONV - 1 - s:SSM_CONV - s, :]
    act = acc * _sigmoid(acc)
    xs_out[...] = act[:, :SSM_DIM]
    gn = SSM_GROUPS * SSM_STATE
    b_out[...] = act[:, SSM_DIM:SSM_DIM + gn].astype(BF16)
    c_out[...] = act[:, SSM_DIM + gn:].astype(BF16)


def _ssd_kernel(xs_ref, b_ref, c_ref, dt_ref, z_ref, alog_ref, d_ref, ng_ref, o_ref, state_ref, y_ref):
    @pl.when(pl.program_id(1) == 0)
    def _():
        state_ref[...] = jnp.zeros_like(state_ref)

    L = xs_ref.shape[0]
    row = _iota2((L, L), 0)
    col = _iota2((L, L), 1)
    lower = row >= col
    tril = lower.astype(F32)
    eye = (_iota2((LANE, LANE), 0) == _iota2((LANE, LANE), 1)).astype(F32)

    dt = dt_ref[...]
    da = dt * (-jnp.exp(alog_ref[...]))
    acs = _dot_exact(tril, da)
    acs_t = _dot_nt_exact(eye, acs)
    tot = acs[L - 1:L, :]
    decay_out = jnp.exp(acs)
    decay_end = jnp.exp(tot - acs)
    decay_tot = jnp.exp(tot)

    for g in range(SSM_GROUPS):
        gs = slice(SSM_STATE * g, SSM_STATE * (g + 1))
        bm = b_ref[:, gs]
        cm = c_ref[:, gs]
        cb = _dot_nt(cm, bm)
        bm_t = _dot_nt(eye, bm)
        for e in range(SSM_HPG):
            hd = SSM_HPG * g + e
            hs = slice(SSM_HEAD * hd, SSM_HEAD * (hd + 1))
            xs = xs_ref[:, hs]
            xdt = xs * dt[:, hd:hd + 1]
            seg = jnp.where(lower, jnp.exp(acs[:, hd:hd + 1] - acs_t[hd:hd + 1, :]), 0.0)
            st = state_ref[hd]
            y = _dot(cb * seg, xdt) + _dot(cm, st) * decay_out[:, hd:hd + 1]
            state_ref[hd] = st * decay_tot[:, hd:hd + 1] + _dot(bm_t, xdt * decay_end[:, hd:hd + 1])
            y_ref[:, hs] = y + xs * d_ref[:, hd:hd + 1]

    z = z_ref[...]
    y = y_ref[...] * (z * _sigmoid(z))
    gw = SSM_DIM // SSM_GROUPS
    for g in range(SSM_GROUPS):
        gs = slice(gw * g, gw * (g + 1))
        yg = y[:, gs]
        yg = yg * lax.rsqrt(jnp.mean(yg * yg, axis=-1, keepdims=True) + SSM_NORM_EPS)
        o_ref[:, gs] = (yg * ng_ref[:, gs]).astype(BF16)


def _ssm_branch(x2, bsz, seq, gmix, wp):
    n_tok = x2.shape[0]
    tm = min(TOK_TILE, seq)
    ns = seq // tm
    gn = SSM_GROUPS * SSM_STATE
    tok = lambda w: pl.BlockSpec((tm, w), lambda b, i: (b * ns + i, 0))
    z, xs, bm, cm, dt = pl.pallas_call(
        _ssm_prep_kernel,
        grid=(bsz, ns),
        in_specs=[tok(D_MODEL), _full((1, D_MODEL)), _full(wp["w_ssm"].shape), _full((SSM_CONV, SSM_CONV_DIM)),
                  _full((1, SSM_CONV_DIM)), _full((1, LANE))],
        out_specs=[tok(SSM_DIM), tok(SSM_DIM), tok(gn), tok(gn), tok(LANE)],
        out_shape=[jax.ShapeDtypeStruct((n_tok, SSM_DIM), F32), jax.ShapeDtypeStruct((n_tok, SSM_DIM), F32),
                   jax.ShapeDtypeStruct((n_tok, gn), BF16), jax.ShapeDtypeStruct((n_tok, gn), BF16),
                   jax.ShapeDtypeStruct((n_tok, LANE), F32)],
        scratch_shapes=[pltpu.VMEM((SUBLANE, SSM_CONV_DIM), F32)],
        compiler_params=_params("parallel", "arbitrary"),
        name="ssm_prep",
    )(x2, gmix, wp["w_ssm"], wp["conv_w"], wp["conv_b"], wp["dt_bias"])

    assert seq % SSM_CHUNK == 0
    nc = seq // SSM_CHUNK
    ch = lambda w: pl.BlockSpec((SSM_CHUNK, w), lambda b, c: (b * nc + c, 0))
    return pl.pallas_call(
        _ssd_kernel,
        grid=(bsz, nc),
        in_specs=[ch(SSM_DIM), ch(gn), ch(gn), ch(LANE), ch(SSM_DIM), _full((1, LANE)), _full((1, LANE)),
                  _full((1, SSM_DIM))],
        out_specs=ch(SSM_DIM),
        out_shape=jax.ShapeDtypeStruct((n_tok, SSM_DIM), BF16),
        scratch_shapes=[pltpu.VMEM((SSM_HEADS, SSM_STATE, SSM_HEAD), F32), pltpu.VMEM((SSM_CHUNK, SSM_DIM), F32)],
        compiler_params=_params("parallel", "arbitrary"),
        name="ssd",
    )(xs, bm, cm, dt, z, wp["a_log"], wp["d_skip"], wp["norm_g"])


def _merge_kernel(x_ref, gmix_ref, wg_ref, om_ref, or_ref, os_ref, wm_ref, wr_ref, ws_ref, wo_ref, o_ref):
    x = x_ref[...]
    h = _rms(x, gmix_ref[...], NORM_EPS).astype(BF16)
    d = D_MODEL
    merged = _sigmoid(_dot(h, wg_ref[:, :d])) * _dot(om_ref[...], wm_ref[...])
    merged = merged + _sigmoid(_dot(h, wg_ref[:, d:2 * d])) * _dot(or_ref[...], wr_ref[...])
    merged = merged + _sigmoid(_dot(h, wg_ref[:, 2 * d:])) * _dot(os_ref[...], ws_ref[...])
    o_ref[...] = x + _dot(merged, wo_ref[...])


def _ffn_kernel(x_ref, g_ref, w1_ref, w2_ref, o_ref, h_ref):
    j = pl.program_id(1)

    @pl.when(j == 0)
    def _():
        x = x_ref[...]
        h_ref[...] = _rms(x, g_ref[...], NORM_EPS).astype(BF16)
        o_ref[...] = x

    u = jnp.maximum(_dot(h_ref[...], w1_ref[...]), 0.0)
    o_ref[...] += _dot(u * u, w2_ref[...])


def _merge_ffn(x2, gmix, wp, o_mla, o_rwkv, o_ssm):
    n_tok = x2.shape[0]
    tm = min(TOK_TILE, n_tok)
    tok = lambda w: pl.BlockSpec((tm, w), lambda i: (i, 0))
    x2 = pl.pallas_call(
        _merge_kernel,
        grid=(n_tok // tm,),
        in_specs=[tok(D_MODEL), _full((1, D_MODEL)), _full((D_MODEL, GATE_IN)), tok(o_mla.shape[1]),
                  tok(o_rwkv.shape[1]), tok(o_ssm.shape[1]), _full(wp["w_br_mla"].shape),
                  _full(wp["w_br_rwkv"].shape), _full(wp["w_br_ssm"].shape), _full((D_MODEL, D_MODEL))],
        out_specs=tok(D_MODEL),
        out_shape=jax.ShapeDtypeStruct((n_tok, D_MODEL), F32),
        compiler_params=_params("parallel"),
        name="merge",
    )(x2, gmix, wp["w_gate"], o_mla, o_rwkv, o_ssm, wp["w_br_mla"], wp["w_br_rwkv"], wp["w_br_ssm"], wp["w_out"])

    tf = min(FFN_TOK_TILE, n_tok)
    ff = FFN_FF_TILE
    return pl.pallas_call(
        _ffn_kernel,
        grid=(n_tok // tf, D_FF // ff),
        in_specs=[pl.BlockSpec((tf, D_MODEL), lambda i, j: (i, 0)), _full((1, D_MODEL)),
                  pl.BlockSpec((D_MODEL, ff), lambda i, j: (0, j)), pl.BlockSpec((ff, D_MODEL), lambda i, j: (j, 0))],
        out_specs=pl.BlockSpec((tf, D_MODEL), lambda i, j: (i, 0)),
        out_shape=jax.ShapeDtypeStruct((n_tok, D_MODEL), F32),
        scratch_shapes=[pltpu.VMEM((tf, D_MODEL), BF16)],
        compiler_params=_params("parallel", "arbitrary"),
        name="ffn",
    )(x2, wp["norm_ffn_g"], wp["w_ff1"], wp["w_ff2"])


def _pad_cols(w, width):
    return jnp.pad(w, ((0, 0), (0, width - w.shape[1])))


def _pad_rows(w, height, before=0):
    return jnp.pad(w, ((before, height - before - w.shape[0]), (0, 0)))


def _row(v, width=None):
    v = v.reshape(1, -1).astype(F32)
    return v if width is None else _pad_cols(v, width)


def _layer_weights(l, p):
    w_in = p["w_in"][l]
    o_r = MLA_IN
    o_s = o_r + RWKV_IN
    o_g = o_s + SSM_IN
    zc = lambda n: jnp.zeros((D_MODEL, n), w_in.dtype)
    w_mla = jnp.concatenate([w_in[:, :MLA_Q_RANK + MLA_KV_RANK], zc(MLA_NOPE), w_in[:, MLA_Q_RANK + MLA_KV_RANK:MLA_IN],
                             zc(LANE - MLA_QK)], axis=1)
    w_uq = p["mla_w_uq"][l].reshape(MLA_Q_RANK, MLA_HEADS, MLA_QK)
    w_uq = jnp.pad(w_uq, ((0, 0), (0, 0), (0, LANE - MLA_QK))).reshape(MLA_Q_RANK, MLA_HEADS * LANE)
    w_ukv = p["mla_w_ukv"][l].reshape(MLA_KV_RANK, MLA_HEADS, MLA_NOPE + MLA_V)
    w_uk = jnp.pad(w_ukv[:, :, :MLA_NOPE], ((0, 0), (0, 0), (0, LANE - MLA_NOPE))).reshape(MLA_KV_RANK, MLA_HEADS * LANE)
    w_uv = w_ukv[:, :, MLA_NOPE:].reshape(MLA_KV_RANK, MLA_HEADS * MLA_V)
    w_ssm = _pad_cols(w_in[:, o_s:o_g], SSM_DIM + SSM_CONV_DIM + LANE)
    wp = {
        "w_mla": w_mla.astype(BF16), "w_uq": w_uq.astype(BF16), "w_uk": w_uk.astype(BF16), "w_uv": w_uv.astype(BF16),
        "q_norm_g": _row(p["mla_q_norm_g"][l]), "kv_norm_g": _row(p["mla_kv_norm_g"][l]),
        "q_head_g": _row(p["mla_q_head_g"][l], LANE), "k_head_g": _row(p["mla_k_head_g"][l], LANE),
        "w_rwkv": w_in[:, o_r:o_s].astype(BF16), "mu": _row(p["rwkv_mu"][l]), "w0": _row(p["rwkv_w0"][l]),
        "w2": _pad_rows(p["rwkv_w2"][l], LANE).astype(BF16), "a0": _row(p["rwkv_a0"][l]),
        "a2": _pad_rows(p["rwkv_a2"][l], LANE, before=RWKV_W_RANK).astype(BF16), "g2": p["rwkv_g2"][l].astype(BF16),
        "k_k": _row(p["rwkv_k_k"][l]), "k_a": _row(p["rwkv_k_a"][l]), "r_k": _row(p["rwkv_r_k"][l]),
        "ln_g": _row(p["rwkv_ln_g"][l]), "ln_b": _row(p["rwkv_ln_b"][l]),
        "w_ssm": w_ssm.astype(BF16), "conv_w": p["ssm_conv_w"][l].astype(F32), "conv_b": _row(p["ssm_conv_b"][l]),
        "dt_bias": _row(p["ssm_dt_bias"][l], LANE), "a_log": _row(p["ssm_a_log"][l], LANE),
        "d_skip": _row(p["ssm_d"][l], LANE), "norm_g": _row(p["ssm_norm_g"][l]),
        "w_gate": w_in[:, o_g:].astype(BF16), "w_br_mla": p["w_br_mla"][l].astype(BF16),
        "w_br_rwkv": p["w_br_rwkv"][l].astype(BF16), "w_br_ssm": p["w_br_ssm"][l].astype(BF16),
        "w_out": p["w_out"][l].astype(BF16), "norm_ffn_g": _row(p["norm_ffn_g"][l]),
        "w_ff1": p["w_ff1"][l].astype(BF16), "w_ff2": p["w_ff2"][l].astype(BF16),
    }
    if l > 0:
        wp["v0"] = _row(p["rwkv_v0"][l - 1])
        wp["v1"] = _pad_cols(p["rwkv_v1"][l - 1], LANE).astype(BF16)
        wp["v2"] = _pad_rows(p["rwkv_v2"][l - 1], LANE).astype(BF16)
    return wp


def kernel(x, positions, norm_mix_g, w_in, mla_q_norm_g, mla_kv_norm_g, mla_w_uq, mla_w_ukv, mla_q_head_g, mla_k_head_g, rwkv_mu, rwkv_w0, rwkv_w2, rwkv_a0, rwkv_a2, rwkv_g2, rwkv_v0, rwkv_v1, rwkv_v2, rwkv_k_k, rwkv_k_a, rwkv_r_k, rwkv_ln_g, rwkv_ln_b, ssm_conv_w, ssm_conv_b, ssm_dt_bias, ssm_a_log, ssm_d, ssm_norm_g, w_br_mla, w_br_rwkv, w_br_ssm, w_out, norm_ffn_g, w_ff1, w_ff2):
    p = dict(w_in=w_in, mla_q_norm_g=mla_q_norm_g, mla_kv_norm_g=mla_kv_norm_g, mla_w_uq=mla_w_uq,
             mla_w_ukv=mla_w_ukv, mla_q_head_g=mla_q_head_g, mla_k_head_g=mla_k_head_g, rwkv_mu=rwkv_mu,
             rwkv_w0=rwkv_w0, rwkv_w2=rwkv_w2, rwkv_a0=rwkv_a0, rwkv_a2=rwkv_a2, rwkv_g2=rwkv_g2, rwkv_v0=rwkv_v0,
             rwkv_v1=rwkv_v1, rwkv_v2=rwkv_v2, rwkv_k_k=rwkv_k_k, rwkv_k_a=rwkv_k_a, rwkv_r_k=rwkv_r_k,
             rwkv_ln_g=rwkv_ln_g, rwkv_ln_b=rwkv_ln_b, ssm_conv_w=ssm_conv_w, ssm_conv_b=ssm_conv_b,
             ssm_dt_bias=ssm_dt_bias, ssm_a_log=ssm_a_log, ssm_d=ssm_d, ssm_norm_g=ssm_norm_g, w_br_mla=w_br_mla,
             w_br_rwkv=w_br_rwkv, w_br_ssm=w_br_ssm, w_out=w_out, norm_ffn_g=norm_ffn_g, w_ff1=w_ff1, w_ff2=w_ff2)
    bsz, seq, d_model = x.shape
    depth = w_in.shape[0]
    x2 = x.reshape(bsz * seq, d_model)
    tables = _rope_tables(positions)
    v_first = None
    for l in range(depth):
        wp = _layer_weights(l, p)
        gmix = _row(norm_mix_g[l])
        o_mla = _mla_branch(x2, bsz, seq, gmix, wp, tables)
        o_rwkv, v = _rwkv_branch(x2, bsz, seq, gmix, wp, v_first)
        if l == 0:
            v_first = v
        o_ssm = _ssm_branch(x2, bsz, seq, gmix, wp)
        x2 = _merge_ffn(x2, gmix, wp, o_mla, o_rwkv, o_ssm)
    return x2.reshape(bsz, seq, d_model)
```

```python
import functools

import jax
import jax.numpy as jnp
from jax import lax
from jax.experimental import pallas as pl
from jax.experimental.pallas import tpu as pltpu

F32 = jnp.float32
BF16 = jnp.bfloat16

D_MODEL = 1024
MLA_HEADS = 8
MLA_NOPE = 64
MLA_ROPE = 32
MLA_QK = MLA_NOPE + MLA_ROPE
MLA_V = 64
MLA_Q_RANK = 384
MLA_KV_RANK = 256
ROPE_THETA = 10000.0
RWKV_HEAD = 64
RWKV_HEADS = 8
RWKV_DIM = RWKV_HEADS * RWKV_HEAD
RWKV_W_RANK = 64
RWKV_A_RANK = 64
RWKV_V_RANK = 32
RWKV_G_RANK = 128
RWKV_GN_EPS = 64e-5
SSM_HEAD = 64
SSM_HEADS = 16
SSM_DIM = SSM_HEADS * SSM_HEAD
SSM_GROUPS = 2
SSM_HPG = SSM_HEADS // SSM_GROUPS
SSM_STATE = 128
SSM_CONV = 4
SSM_CHUNK = 256
SSM_CONV_DIM = SSM_DIM + 2 * SSM_GROUPS * SSM_STATE
SSM_NORM_EPS = 1e-5
N_BRANCH = 3
D_FF = 4 * D_MODEL
NORM_EPS = 1e-6
MLA_IN = MLA_Q_RANK + MLA_KV_RANK + MLA_ROPE
RWKV_IN = 3 * RWKV_DIM + RWKV_W_RANK + RWKV_A_RANK + RWKV_G_RANK
SSM_IN = SSM_DIM + SSM_CONV_DIM + SSM_HEADS
GATE_IN = N_BRANCH * D_MODEL

LOG2_E = 1.4426950408889634
LANE = 128
SUBLANE = 8
VMEM_LIMIT = 56 * 1024 * 1024

TOK_TILE = 512
FFN_TOK_TILE = 1024
FFN_FF_TILE = 1024
ATTN_TILE = 512
ATTN_HEADS = 8
ATTN_KEYS = 256
SSM_CONV_GROUP = 512
RWKV_CHUNK = 64
RWKV_TILE = 512
RWKV_SCAN_HEADS = 4


def _dot(a, b):
    return jnp.dot(a.astype(BF16), b.astype(BF16), preferred_element_type=F32)


def _dot_nt(a, b):
    return lax.dot_general(a.astype(BF16), b.astype(BF16), (((1,), (1,)), ((), ())),
                           preferred_element_type=F32)


def _dot_tn(a, b):
    return lax.dot_general(a.astype(BF16), b.astype(BF16), (((0,), (0,)), ((), ())),
                           preferred_element_type=F32)


def _bmm(a, b):
    return lax.dot_general(a.astype(BF16), b.astype(BF16), (((2,), (1,)), ((0,), (0,))),
                           preferred_element_type=F32)


def _bmm_nt(a, b):
    return lax.dot_general(a.astype(BF16), b.astype(BF16), (((2,), (2,)), ((0,), (0,))),
                           preferred_element_type=F32)


def _bmm_tn(a, b):
    return lax.dot_general(a.astype(BF16), b.astype(BF16), (((1,), (1,)), ((0,), (0,))),
                           preferred_element_type=F32)


def _dot_exact(a, b):
    return jnp.dot(a, b, preferred_element_type=F32, precision=lax.Precision.HIGHEST)


def _dot_nt_exact(a, b):
    return lax.dot_general(a, b, (((1,), (1,)), ((), ())), preferred_element_type=F32,
                           precision=lax.Precision.HIGHEST)


def _rms(x, g, eps):
    return x * lax.rsqrt(jnp.mean(x * x, axis=-1, keepdims=True) + eps) * g


def _sigmoid(x):
    return 1.0 / (1.0 + jnp.exp(-x))


def _softplus(x):
    return jnp.maximum(x, 0.0) + jnp.log(1.0 + jnp.exp(-jnp.abs(x)))


def _iota2(shape, dim):
    return lax.broadcasted_iota(jnp.int32, shape, dim)


def _params(*sem):
    return pltpu.CompilerParams(dimension_semantics=sem, vmem_limit_bytes=VMEM_LIMIT)


def _full(shape):
    return pl.BlockSpec(shape, lambda *_: (0,) * len(shape))


def _rope_table_kernel(pos_ref, freq_ref, cos_ref, sin_ref):
    ang = pos_ref[...].astype(F32) * freq_ref[...]
    cos_ref[...] = jnp.cos(ang)
    sin_ref[...] = jnp.sin(ang)


def _rope_tables(positions):
    half = MLA_ROPE // 2
    n_tok = positions.size
    inv_freq = ROPE_THETA ** (-jnp.arange(half, dtype=jnp.float32) / half)
    per_row = LANE // half
    rows = n_tok // per_row
    pos = jnp.repeat(positions.reshape(rows, per_row), half, axis=1)
    freq = jnp.tile(inv_freq, per_row).reshape(1, LANE)
    blk = min(rows, 512)
    cos, sin = pl.pallas_call(
        _rope_table_kernel,
        grid=(rows // blk,),
        in_specs=[pl.BlockSpec((blk, LANE), lambda i: (i, 0)), _full((1, LANE))],
        out_specs=[pl.BlockSpec((blk, LANE), lambda i: (i, 0))] * 2,
        out_shape=[jax.ShapeDtypeStruct((rows, LANE), F32)] * 2,
        compiler_params=_params("parallel"),
        name="rope_table",
    )(pos, freq)
    cos = cos.reshape(n_tok, half)
    sin = sin.reshape(n_tok, half)
    one = jnp.ones((n_tok, MLA_NOPE), F32)
    z_nope = jnp.zeros((n_tok, MLA_NOPE), F32)
    z_half = jnp.zeros((n_tok, half), F32)
    z_pad = jnp.zeros((n_tok, LANE - MLA_QK), F32)
    cosf = jnp.concatenate([one, cos, cos, z_pad], axis=1)
    sina = jnp.concatenate([z_nope, -sin, z_half, z_pad], axis=1)
    sinb = jnp.concatenate([z_nope, z_half, sin, z_pad], axis=1)
    return cosf, sina, sinb


def _mla_prep_kernel(x_ref, gmix_ref, wmla_ref, qg_ref, kvg_ref, wuq_ref, wuk_ref, wuv_ref,
                     qhg_ref, khg_ref, cos_ref, sina_ref, sinb_ref, q_ref, k_ref, v_ref):
    h = _rms(x_ref[...], gmix_ref[...], NORM_EPS)
    c = _dot(h, wmla_ref[...])
    cq = _rms(c[:, :MLA_Q_RANK], qg_ref[...], NORM_EPS)
    ckv = _rms(c[:, MLA_Q_RANK:MLA_Q_RANK + MLA_KV_RANK], kvg_ref[...], NORM_EPS)
    krope = c[:, MLA_Q_RANK + MLA_KV_RANK:]
    hq = MLA_HEADS * LANE
    qall = _dot(cq, wuq_ref[...])
    kf = _dot(ckv, wuk_ref[...])
    v_ref[...] = _dot(ckv, wuv_ref[...]).astype(BF16)
    cosf, sina, sinb = cos_ref[...], sina_ref[...], sinb_ref[...]
    khg = khg_ref[...]
    scale = MLA_QK ** -0.5 * LOG2_E

    def inv_rms(t):
        return lax.rsqrt(jnp.sum(t * t, axis=-1, keepdims=True) * (1.0 / MLA_QK) + NORM_EPS)

    q_cos = qhg_ref[0:1, :] * cosf
    q_sin = qhg_ref[1:2, :] * (sinb - sina)
    k_cos = khg * cosf
    kg = krope * khg
    k_rot = pltpu.roll(kg, LANE - MLA_ROPE // 2, 1) * sina + pltpu.roll(kg, MLA_ROPE // 2, 1) * sinb

    for hd in range(MLA_HEADS):
        sl = slice(LANE * hd, LANE * (hd + 1))
        qh = qall[:, sl]
        qr = qall[:, hq + LANE * hd:hq + LANE * (hd + 1)]
        q_ref[:, sl] = ((qh * q_cos + qr * q_sin) * (inv_rms(qh) * scale)).astype(BF16)
        kh = kf[:, sl] + krope
        k_ref[:, sl] = ((kh * k_cos + k_rot) * inv_rms(kh)).astype(BF16)


def _attn_kernel(q_ref, k_ref, v_ref, o_ref, m_ref, l_ref, acc_ref):
    i = pl.program_id(2)
    tq = q_ref.shape[0]
    heads = range(ATTN_HEADS)
    m_ref[...] = jnp.full(m_ref.shape, -jnp.inf, F32)
    l_ref[...] = jnp.zeros(l_ref.shape, F32)
    acc_ref[...] = jnp.zeros(acc_ref.shape, F32)

    def step(j, masked):
        for sub in range(tq // ATTN_KEYS):
            substep(j * tq + sub * ATTN_KEYS, sub * ATTN_KEYS, masked)

    def substep(start, offset, masked):
        rows = pl.ds(pl.multiple_of(start, ATTN_KEYS), ATTN_KEYS)
        scores = [_dot_nt(k_ref[rows, LANE * hh:LANE * (hh + 1)], q_ref[:, LANE * hh:LANE * (hh + 1)])
                  for hh in heads]
        probs = []
        for hh in heads:
            s = scores[hh]
            if masked:
                s = jnp.where(_iota2(s.shape, 1) >= _iota2(s.shape, 0) + offset, s, -jnp.inf)
            m = m_ref[hh]
            m_new = jnp.maximum(m, jnp.max(s, axis=0, keepdims=True))
            p = jnp.exp2(s - m_new)
            alpha = jnp.exp2(m - m_new)
            m_ref[hh] = m_new
            l_ref[hh] = alpha * l_ref[hh] + jnp.sum(p, axis=0, keepdims=True)
            probs.append((p.astype(BF16), alpha))
        for hh in heads:
            p, alpha = probs[hh]
            v = v_ref[rows, MLA_V * hh:MLA_V * (hh + 1)]
            acc_ref[hh] = alpha * acc_ref[hh] + _dot_tn(v, p)

    def body(j, carry):
        step(j, False)
        return carry

    lax.fori_loop(0, i, body, 0)
    step(i, True)
    out = jnp.concatenate([acc_ref[hh] / l_ref[hh] for hh in heads], axis=0)
    o_ref[...] = out.T.astype(BF16)


def _mla_branch(x2, bsz, seq, gmix, wp, tables):
    n_tok = x2.shape[0]
    tm = min(TOK_TILE, seq)
    cosf, sina, sinb = tables
    tok = lambda w: pl.BlockSpec((tm, w), lambda i: (i, 0))
    hq = MLA_HEADS * LANE
    q, k, v = pl.pallas_call(
        _mla_prep_kernel,
        grid=(n_tok // tm,),
        in_specs=[tok(D_MODEL), _full((1, D_MODEL)), _full(wp["w_mla"].shape), _full((1, MLA_Q_RANK)),
                  _full((1, MLA_KV_RANK)), _full(wp["w_uq"].shape), _full(wp["w_uk"].shape),
                  _full(wp["w_uv"].shape), _full((2, LANE)), _full((1, LANE)), tok(LANE), tok(LANE), tok(LANE)],
        out_specs=[tok(hq), tok(hq), tok(MLA_HEADS * MLA_V)],
        out_shape=[jax.ShapeDtypeStruct((n_tok, hq), BF16), jax.ShapeDtypeStruct((n_tok, hq), BF16),
                   jax.ShapeDtypeStruct((n_tok, MLA_HEADS * MLA_V), BF16)],
        compiler_params=_params("parallel"),
        name="mla_prep",
    )(x2, gmix, wp["w_mla"], wp["q_norm_g"], wp["kv_norm_g"], wp["w_uq"], wp["w_uk"], wp["w_uv"],
      wp["q_head_g"], wp["k_head_g"], cosf, sina, sinb)

    tq = min(ATTN_TILE, seq)
    nq = seq // tq
    return pl.pallas_call(
        _attn_kernel,
        grid=(bsz, MLA_HEADS // ATTN_HEADS, nq),
        in_specs=[pl.BlockSpec((tq, ATTN_HEADS * LANE), lambda b, hp, i: (b * nq + i, hp)),
                  pl.BlockSpec((seq, ATTN_HEADS * LANE), lambda b, hp, i: (b, hp)),
                  pl.BlockSpec((seq, ATTN_HEADS * MLA_V), lambda b, hp, i: (b, hp))],
        out_specs=pl.BlockSpec((tq, ATTN_HEADS * MLA_V), lambda b, hp, i: (b * nq + i, hp)),
        out_shape=jax.ShapeDtypeStruct((n_tok, MLA_HEADS * MLA_V), BF16),
        scratch_shapes=[pltpu.VMEM((ATTN_HEADS, 1, tq), F32), pltpu.VMEM((ATTN_HEADS, 1, tq), F32),
                        pltpu.VMEM((ATTN_HEADS, MLA_V, tq), F32)],
        compiler_params=_params("parallel", "parallel", "arbitrary"),
        name="mla_attention",
    )(q, k, v)


def _rwkv_prep_kernel(has_vres, *refs):
    if has_vres:
        (x_ref, gmix_ref, w_ref, mu_ref, w0_ref, w2_ref, a0_ref, a2_ref, g2_ref, kk_ref, ka_ref,
         vfirst_ref, v0_ref, v1_ref, v2_ref,
         r_out, lw_out, k_out, v_out, kk_out, a_out, g_out, carry_ref) = refs
    else:
        (x_ref, gmix_ref, w_ref, mu_ref, w0_ref, w2_ref, a0_ref, a2_ref, g2_ref, kk_ref, ka_ref,
         r_out, lw_out, k_out, v_out, kk_out, a_out, g_out, carry_ref) = refs

    @pl.when(pl.program_id(1) == 0)
    def _():
        carry_ref[...] = jnp.zeros_like(carry_ref)

    h = _rms(x_ref[...], gmix_ref[...], NORM_EPS)
    p = _dot(h, w_ref[...])
    tm = p.shape[0]
    rolled = pltpu.roll(p, 1, 0)
    prev = jnp.where(_iota2(p.shape, 0) == 0, carry_ref[...], rolled)
    carry_ref[...] = p[tm - 1:tm, :]
    p = p + (prev - p) * mu_ref[...]

    d = RWKV_DIM
    r, k, v = p[:, :d], p[:, d:2 * d], p[:, 2 * d:3 * d]
    xwa = p[:, 3 * d:3 * d + LANE]
    xg = p[:, 3 * d + LANE:]
    w = -_softplus(-(w0_ref[...] + _dot(jnp.tanh(xwa), w2_ref[...]))) - 0.5
    lw_out[...] = -jnp.exp(w)
    if has_vres:
        gate = _sigmoid(v0_ref[...] + _dot(_dot(v, v1_ref[...]), v2_ref[...]))
        v = v + (vfirst_ref[...] - v) * gate
    a = _sigmoid(a0_ref[...] + _dot(xwa, a2_ref[...]))
    g_out[...] = _dot(_sigmoid(xg), g2_ref[...])
    r_out[...] = r
    kk_out[...] = k * kk_ref[...]
    k_out[...] = k * (1.0 + (a - 1.0) * ka_ref[...])
    v_out[...] = v
    a_out[...] = a


def _head_sum(x):
    blocks = []
    for j in range(x.shape[-1] // LANE):
        xb = x[..., LANE * j:LANE * (j + 1)]
        first = _iota2(xb.shape, xb.ndim - 1) < RWKV_HEAD
        s0 = jnp.sum(jnp.where(first, xb, 0.0), axis=-1, keepdims=True)
        s1 = jnp.sum(jnp.where(first, 0.0, xb), axis=-1, keepdims=True)
        blocks.append(jnp.where(first, s0, s1))
    return blocks[0] if len(blocks) == 1 else jnp.concatenate(blocks, axis=-1)


def _rwkv_scan_kernel(r_ref, lw_ref, k_ref, v_ref, kk_ref, a_ref, g_ref, rk_ref, lng_ref, lnb_ref,
                      o_ref, state_ref, y_ref):
    L = RWKV_CHUNK
    n = RWKV_HEAD
    ts, width = r_ref.shape
    nh = width // n
    nchunk = ts // L
    sh3 = (nchunk, L, width)

    @pl.when(pl.program_id(2) == 0)
    def _():
        state_ref[...] = jnp.zeros_like(state_ref)

    t_idx = _iota2(sh3, 1)
    r = r_ref[...].reshape(sh3)
    lw = lw_ref[...].reshape(sh3)
    k = k_ref[...].reshape(sh3)
    v = v_ref[...].reshape(sh3)
    kk = kk_ref[...].reshape(sh3)
    a = a_ref[...].reshape(sh3)

    kk = kk / jnp.maximum(jnp.sqrt(_head_sum(kk * kk)), 1e-12)
    cum = lw
    s = 1
    while s < L:
        cum = cum + jnp.where(t_idx >= s, pltpu.roll(cum, s, 1), 0.0)
        s *= 2
    tot = cum[:, L - 1:L, :]
    inv = jnp.exp(-cum)
    to_end = jnp.exp(tot - cum)

    def heads_to_batch(x):
        return jnp.concatenate([x[:, :, n * hh:n * (hh + 1)] for hh in range(nh)], axis=0)

    at = heads_to_batch(-kk * jnp.exp(cum - lw))
    rt = heads_to_batch(r * jnp.exp(cum))
    bt = heads_to_batch(kk * a * inv)
    kt = heads_to_batch(k * inv)
    bend = heads_to_batch(kk * a * to_end).astype(BF16)
    kend = heads_to_batch(k * to_end)
    vb = heads_to_batch(v)
    tot2 = tot.reshape(nchunk, width)
    pad = jnp.zeros((LANE - nchunk, LANE), F32)
    gam_t = [jnp.exp(jnp.concatenate([tot2[:, LANE * j:LANE * (j + 1)], pad], axis=0).T)
             for j in range(width // LANE)]

    nb = nh * nchunk
    row = _iota2((nb, 2 * L, L), 1)
    col = _iota2((nb, 2 * L, L), 2)
    causal = row - col >= jnp.where(row < L, 1, L)
    eye = (_iota2((nb, L, L), 1) == _iota2((nb, L, L), 2)).astype(F32)

    ar = jnp.concatenate([at, rt], axis=1)
    pb = jnp.where(causal, _bmm_nt(ar, bt), 0.0)
    pk = jnp.where(causal, _bmm_nt(ar, kt), 0.0)
    wv = _bmm(pk, vb)
    mab = pb[:, :L]
    tinv = eye + mab
    pw = _bmm(mab, mab)
    s = 4
    while s < L:
        both = _bmm(jnp.concatenate([tinv, pw], axis=1), pw)
        tinv = tinv + both[:, :L]
        pw = both[:, L:]
        s *= 2
    tinv = tinv + _bmm(tinv, pw)
    abar = _bmm(tinv, at)
    u0 = _bmm(tinv, wv[:, :L])
    wmat = _bmm_tn(bend, abar).astype(BF16)
    nmat = _bmm_tn(bend, u0) + _bmm_tn(kend, vb)
    ar = jnp.concatenate([abar, rt], axis=1).astype(BF16)
    nrb = pb[:, L:].astype(BF16)
    y0 = wv[:, L:]

    hs = range(nh)
    st = [state_ref[hh] for hh in hs]
    for c in range(nchunk):
        idx = [hh * nchunk + c for hh in hs]
        upd = [_dot(wmat[idx[hh]], st[hh]) for hh in hs]
        rd = [_dot(ar[idx[hh]], st[hh]) for hh in hs]
        for hh in hs:
            u = rd[hh][:L] + u0[idx[hh]]
            y_ref[L * c:L * (c + 1), n * hh:n * (hh + 1)] = rd[hh][L:] + _dot(nrb[idx[hh]], u) + y0[idx[hh]]
            gcol = gam_t[hh // 2][n * (hh % 2):n * (hh % 2 + 1), c:c + 1]
            st[hh] = st[hh] * gcol + upd[hh] + nmat[idx[hh]]
    for hh in hs:
        state_ref[hh] = st[hh]

    y = y_ref[...]
    mean = _head_sum(y) * (1.0 / n)
    var = _head_sum(jnp.square(y - mean)) * (1.0 / n)
    y = (y - mean) * lax.rsqrt(var + RWKV_GN_EPS)
    y = y * lng_ref[...] + lnb_ref[...]
    y = y + _head_sum(r_ref[...] * k_ref[...] * rk_ref[...]) * v_ref[...]
    o_ref[...] = (y * g_ref[...]).astype(BF16)


def _rwkv_branch(x2, bsz, seq, gmix, wp, v_first):
    n_tok = x2.shape[0]
    tm = min(TOK_TILE, seq)
    ns = seq // tm
    has_vres = v_first is not None
    d = RWKV_DIM
    tok = lambda w: pl.BlockSpec((tm, w), lambda b, i: (b * ns + i, 0))
    vec = _full((1, d))
    in_specs = [tok(D_MODEL), _full((1, D_MODEL)), _full((D_MODEL, RWKV_IN)), _full((1, RWKV_IN)), vec,
                _full((LANE, d)), vec, _full((LANE, d)), _full((RWKV_G_RANK, d)), vec, vec]
    args = [x2, gmix, wp["w_rwkv"], wp["mu"], wp["w0"], wp["w2"], wp["a0"], wp["a2"], wp["g2"], wp["k_k"], wp["k_a"]]
    if has_vres:
        in_specs += [tok(d), vec, _full((d, LANE)), _full((LANE, d))]
        args += [v_first, wp["v0"], wp["v1"], wp["v2"]]
    r, lw, k, v, kk, a, g = pl.pallas_call(
        functools.partial(_rwkv_prep_kernel, has_vres),
        grid=(bsz, ns),
        in_specs=in_specs,
        out_specs=[tok(d)] * 7,
        out_shape=[jax.ShapeDtypeStruct((n_tok, d), F32)] * 7,
        scratch_shapes=[pltpu.VMEM((1, RWKV_IN), F32)],
        compiler_params=_params("parallel", "arbitrary"),
        name="rwkv_prep",
    )(*args)

    ts = min(RWKV_TILE, seq)
    nt = seq // ts
    width = RWKV_SCAN_HEADS * RWKV_HEAD
    blk = pl.BlockSpec((ts, width), lambda b, hp, i: (b * nt + i, hp))
    par = pl.BlockSpec((1, width), lambda b, hp, i: (0, hp))
    o = pl.pallas_call(
        _rwkv_scan_kernel,
        grid=(bsz, RWKV_HEADS // RWKV_SCAN_HEADS, nt),
        in_specs=[blk] * 7 + [par] * 3,
        out_specs=blk,
        out_shape=jax.ShapeDtypeStruct((n_tok, d), BF16),
        scratch_shapes=[pltpu.VMEM((RWKV_SCAN_HEADS, RWKV_HEAD, RWKV_HEAD), F32), pltpu.VMEM((ts, width), F32)],
        compiler_params=_params("parallel", "parallel", "arbitrary"),
        name="rwkv_scan",
    )(r, lw, k, v, kk, a, g, wp["r_k"], wp["ln_g"], wp["ln_b"])
    return o, v


def _ssm_prep_kernel(x_ref, gmix_ref, w_ref, cw_ref, cb_ref, dtb_ref,
                     z_out, xs_out, b_out, c_out, dt_out, carry_ref):
    @pl.when(pl.program_id(1) == 0)
    def _():
        carry_ref[...] = jnp.zeros_like(carry_ref)

    h = _rms(x_ref[...], gmix_ref[...], NORM_EPS).astype(BF16)
    tm = h.shape[0]
    gw = SSM_CONV_GROUP
    top_row = _iota2((SUBLANE, gw), 0)
    for j in range(SSM_CONV_DIM // gw):
        cs = slice(gw * j, gw * (j + 1))
        xbc = _dot(h, w_ref[:, SSM_DIM + gw * j:SSM_DIM + gw * (j + 1)])
        if gw * j < SSM_DIM:
            z_out[:, cs] = _dot(h, w_ref[:, cs])
        else:
            dt_out[...] = _softplus(_dot(h, w_ref[:, SSM_DIM + SSM_CONV_DIM:]) + dtb_ref[...])
        prev = carry_ref[:, cs]
        carry_ref[:, cs] = xbc[tm - SUBLANE:, :]
        acc = xbc * cw_ref[SSM_CONV - 1:SSM_CONV, cs] + cb_ref[:, cs]
        for s in range(1, SSM_CONV):
            rolled = pltpu.roll(xbc, s, 0)
            top = jnp.where(top_row < s, pltpu.roll(prev, s, 0), rolled[:SUBLANE])
            shifted = jnp.concatenate([top, rolled[SUBLANE:]], axis=0)
            acc = acc + shifted * cw_ref[SSM_CONV - 1 - s:SSM_CONV - s, cs]
        act = acc * _sigmoid(acc)
        if gw * j < SSM_DIM:
            xs_out[:, cs] = act
        else:
            gn = SSM_GROUPS * SSM_STATE
            b_out[...] = act[:, :gn].astype(BF16)
            c_out[...] = act[:, gn:].astype(BF16)


def _ssd_kernel(xs_ref, b_ref, c_ref, dt_ref, z_ref, alog_ref, d_ref, ng_ref, o_ref, state_ref, y_ref):
    @pl.when(pl.program_id(1) == 0)
    def _():
        state_ref[...] = jnp.zeros_like(state_ref)

    L = xs_ref.shape[0]
    row = _iota2((L, L), 0)
    col = _iota2((L, L), 1)
    lower = row >= col
    tril = lower.astype(F32)
    eye = (_iota2((LANE, LANE), 0) == _iota2((LANE, LANE), 1)).astype(F32)

    dt = dt_ref[...]
    da = dt * (-jnp.exp(alog_ref[...]))
    acs = _dot_exact(tril, da)
    acs_t = _dot_nt_exact(eye, acs)
    dt_t = _dot_nt_exact(eye, dt)
    first = _iota2((1, LANE), 1) < SSM_HEAD

    for g in range(SSM_GROUPS):
        gs = slice(SSM_STATE * g, SSM_STATE * (g + 1))
        bm = b_ref[:, gs]
        cm = c_ref[:, gs]
        cb = _dot_nt(cm, bm)
        bm_t = _dot_nt(eye, bm)
        for e in range(0, SSM_HPG, 2):
            pair = (SSM_HPG * g + e) // 2
            ps = slice(LANE * pair, LANE * (pair + 1))
            xs = xs_ref[:, ps]
            xs16 = xs.astype(BF16)
            st = state_ref[pair]
            y_in = _dot(cm, st)
            ys, sts = [], []
            for hd in (2 * pair, 2 * pair + 1):
                a_col = jnp.broadcast_to(acs[:, hd:hd + 1], (L, LANE))
                a_row = acs_t[hd:hd + 1, :]
                dt_row = dt_t[hd:hd + 1, :]
                tot = acs[L - 1:L, hd:hd + 1]
                seg = jnp.concatenate([jnp.exp(a_col - a_row[:, LANE * j:LANE * (j + 1)])
                                       for j in range(L // LANE)], axis=1)
                w = cb * jnp.where(lower, seg, 0.0) * dt_row
                ys.append(_dot(w, xs16) + y_in * jnp.exp(a_col))
                sts.append(st * jnp.exp(tot) + _dot(bm_t * (dt_row * jnp.exp(tot - a_row)), xs16))
            y_ref[:, ps] = jnp.where(first, ys[0], ys[1]) + xs * d_ref[:, ps]
            state_ref[pair] = jnp.where(first, sts[0], sts[1])

    z = z_ref[...]
    y = y_ref[...] * (z * _sigmoid(z))
    gw = SSM_DIM // SSM_GROUPS
    for g in range(SSM_GROUPS):
        gs = slice(gw * g, gw * (g + 1))
        yg = y[:, gs]
        yg = yg * lax.rsqrt(jnp.mean(yg * yg, axis=-1, keepdims=True) + SSM_NORM_EPS)
        o_ref[:, gs] = (yg * ng_ref[:, gs]).astype(BF16)


def _ssm_branch(x2, bsz, seq, gmix, wp):
    n_tok = x2.shape[0]
    tm = min(TOK_TILE, seq)
    ns = seq // tm
    gn = SSM_GROUPS * SSM_STATE
    tok = lambda w: pl.BlockSpec((tm, w), lambda b, i: (b * ns + i, 0))
    z, xs, bm, cm, dt = pl.pallas_call(
        _ssm_prep_kernel,
        grid=(bsz, ns),
        in_specs=[tok(D_MODEL), _full((1, D_MODEL)), _full(wp["w_ssm"].shape), _full((SSM_CONV, SSM_CONV_DIM)),
                  _full((1, SSM_CONV_DIM)), _full((1, LANE))],
        out_specs=[tok(SSM_DIM), tok(SSM_DIM), tok(gn), tok(gn), tok(LANE)],
        out_shape=[jax.ShapeDtypeStruct((n_tok, SSM_DIM), F32), jax.ShapeDtypeStruct((n_tok, SSM_DIM), F32),
                   jax.ShapeDtypeStruct((n_tok, gn), BF16), jax.ShapeDtypeStruct((n_tok, gn), BF16),
                   jax.ShapeDtypeStruct((n_tok, LANE), F32)],
        scratch_shapes=[pltpu.VMEM((SUBLANE, SSM_CONV_DIM), F32)],
        compiler_params=_params("parallel", "arbitrary"),
        name="ssm_prep",
    )(x2, gmix, wp["w_ssm"], wp["conv_w"], wp["conv_b"], wp["dt_bias"])

    assert seq % SSM_CHUNK == 0
    nc = seq // SSM_CHUNK
    ch = lambda w: pl.BlockSpec((SSM_CHUNK, w), lambda b, c: (b * nc + c, 0))
    return pl.pallas_call(
        _ssd_kernel,
        grid=(bsz, nc),
        in_specs=[ch(SSM_DIM), ch(gn), ch(gn), ch(LANE), ch(SSM_DIM), _full((1, LANE)), _full((1, SSM_DIM)),
                  _full((1, SSM_DIM))],
        out_specs=ch(SSM_DIM),
        out_shape=jax.ShapeDtypeStruct((n_tok, SSM_DIM), BF16),
        scratch_shapes=[pltpu.VMEM((SSM_HEADS // 2, SSM_STATE, 2 * SSM_HEAD), F32),
                        pltpu.VMEM((SSM_CHUNK, SSM_DIM), F32)],
        compiler_params=_params("parallel", "arbitrary"),
        name="ssd",
    )(xs, bm, cm, dt, z, wp["a_log"], wp["d_skip"], wp["norm_g"])


def _merge_kernel(x_ref, gmix_ref, wg_ref, om_ref, or_ref, os_ref, wm_ref, wr_ref, ws_ref, wo_ref, o_ref):
    x = x_ref[...]
    h = _rms(x, gmix_ref[...], NORM_EPS).astype(BF16)
    d = D_MODEL
    merged = _sigmoid(_dot(h, wg_ref[:, :d])) * _dot(om_ref[...], wm_ref[...])
    merged = merged + _sigmoid(_dot(h, wg_ref[:, d:2 * d])) * _dot(or_ref[...], wr_ref[...])
    merged = merged + _sigmoid(_dot(h, wg_ref[:, 2 * d:])) * _dot(os_ref[...], ws_ref[...])
    o_ref[...] = x + _dot(merged, wo_ref[...])


def _ffn_kernel(x_ref, g_ref, w1_ref, w2_ref, o_ref, h_ref):
    j = pl.program_id(1)

    @pl.when(j == 0)
    def _():
        x = x_ref[...]
        h_ref[...] = _rms(x, g_ref[...], NORM_EPS).astype(BF16)
        o_ref[...] = x

    u = jnp.maximum(_dot(h_ref[...], w1_ref[...]), 0.0)
    o_ref[...] += _dot(u * u, w2_ref[...])


def _merge_ffn(x2, gmix, wp, o_mla, o_rwkv, o_ssm):
    n_tok = x2.shape[0]
    tm = min(TOK_TILE, n_tok)
    tok = lambda w: pl.BlockSpec((tm, w), lambda i: (i, 0))
    x2 = pl.pallas_call(
        _merge_kernel,
        grid=(n_tok // tm,),
        in_specs=[tok(D_MODEL), _full((1, D_MODEL)), _full((D_MODEL, GATE_IN)), tok(o_mla.shape[1]),
                  tok(o_rwkv.shape[1]), tok(o_ssm.shape[1]), _full(wp["w_br_mla"].shape),
                  _full(wp["w_br_rwkv"].shape), _full(wp["w_br_ssm"].shape), _full((D_MODEL, D_MODEL))],
        out_specs=tok(D_MODEL),
        out_shape=jax.ShapeDtypeStruct((n_tok, D_MODEL), F32),
        compiler_params=_params("parallel"),
        name="merge",
    )(x2, gmix, wp["w_gate"], o_mla, o_rwkv, o_ssm, wp["w_br_mla"], wp["w_br_rwkv"], wp["w_br_ssm"], wp["w_out"])

    tf = min(FFN_TOK_TILE, n_tok)
    ff = FFN_FF_TILE
    return pl.pallas_call(
        _ffn_kernel,
        grid=(n_tok // tf, D_FF // ff),
        in_specs=[pl.BlockSpec((tf, D_MODEL), lambda i, j: (i, 0)), _full((1, D_MODEL)),
                  pl.BlockSpec((D_MODEL, ff), lambda i, j: (0, j)), pl.BlockSpec((ff, D_MODEL), lambda i, j: (j, 0))],
        out_specs=pl.BlockSpec((tf, D_MODEL), lambda i, j: (i, 0)),
        out_shape=jax.ShapeDtypeStruct((n_tok, D_MODEL), F32),
        scratch_shapes=[pltpu.VMEM((tf, D_MODEL), BF16)],
        compiler_params=_params("parallel", "arbitrary"),
        name="ffn",
    )(x2, wp["norm_ffn_g"], wp["w_ff1"], wp["w_ff2"])


def _pad_cols(w, width):
    return jnp.pad(w, ((0, 0), (0, width - w.shape[1])))


def _pad_rows(w, height, before=0):
    return jnp.pad(w, ((before, height - before - w.shape[0]), (0, 0)))


def _row(v, width=None):
    v = v.reshape(1, -1).astype(F32)
    return v if width is None else _pad_cols(v, width)


def _layer_weights(l, p):
    w_in = p["w_in"][l]
    o_r = MLA_IN
    o_s = o_r + RWKV_IN
    o_g = o_s + SSM_IN
    zc = lambda n: jnp.zeros((D_MODEL, n), w_in.dtype)
    w_mla = jnp.concatenate([w_in[:, :MLA_Q_RANK + MLA_KV_RANK], zc(MLA_NOPE), w_in[:, MLA_Q_RANK + MLA_KV_RANK:MLA_IN],
                             zc(LANE - MLA_QK)], axis=1)
    half = MLA_ROPE // 2

    def slot(t, rotate):
        lo, hi = t[..., MLA_NOPE:MLA_NOPE + half], t[..., MLA_NOPE + half:]
        parts = [jnp.zeros_like(t[..., :MLA_NOPE]), -hi if rotate == "signed" else hi, lo] if rotate else [t]
        t = jnp.concatenate(parts, axis=-1)
        return jnp.pad(t, [(0, 0)] * (t.ndim - 1) + [(0, LANE - MLA_QK)])

    w_uq = p["mla_w_uq"][l].reshape(MLA_Q_RANK, MLA_HEADS, MLA_QK)
    w_uq = jnp.concatenate([slot(w_uq, None).reshape(MLA_Q_RANK, MLA_HEADS * LANE),
                            slot(w_uq, "signed").reshape(MLA_Q_RANK, MLA_HEADS * LANE)], axis=1)
    q_head_g = p["mla_q_head_g"][l].astype(F32)
    q_head_g = jnp.stack([slot(q_head_g, None), slot(q_head_g, "moved")])
    w_ukv = p["mla_w_ukv"][l].reshape(MLA_KV_RANK, MLA_HEADS, MLA_NOPE + MLA_V)
    w_uk = jnp.pad(w_ukv[:, :, :MLA_NOPE], ((0, 0), (0, 0), (0, LANE - MLA_NOPE))).reshape(MLA_KV_RANK, MLA_HEADS * LANE)
    w_uv = w_ukv[:, :, MLA_NOPE:].reshape(MLA_KV_RANK, MLA_HEADS * MLA_V)
    w_ssm = _pad_cols(w_in[:, o_s:o_g], SSM_DIM + SSM_CONV_DIM + LANE)
    wp = {
        "w_mla": w_mla.astype(BF16), "w_uq": w_uq.astype(BF16), "w_uk": w_uk.astype(BF16), "w_uv": w_uv.astype(BF16),
        "q_norm_g": _row(p["mla_q_norm_g"][l]), "kv_norm_g": _row(p["mla_kv_norm_g"][l]),
        "q_head_g": q_head_g, "k_head_g": _row(p["mla_k_head_g"][l], LANE),
        "w_rwkv": w_in[:, o_r:o_s].astype(BF16), "mu": _row(p["rwkv_mu"][l]), "w0": _row(p["rwkv_w0"][l]),
        "w2": _pad_rows(p["rwkv_w2"][l], LANE).astype(BF16), "a0": _row(p["rwkv_a0"][l]),
        "a2": _pad_rows(p["rwkv_a2"][l], LANE, before=RWKV_W_RANK).astype(BF16), "g2": p["rwkv_g2"][l].astype(BF16),
        "k_k": _row(p["rwkv_k_k"][l]), "k_a": _row(p["rwkv_k_a"][l]), "r_k": _row(p["rwkv_r_k"][l]),
        "ln_g": _row(p["rwkv_ln_g"][l]), "ln_b": _row(p["rwkv_ln_b"][l]),
        "w_ssm": w_ssm.astype(BF16), "conv_w": p["ssm_conv_w"][l].astype(F32), "conv_b": _row(p["ssm_conv_b"][l]),
        "dt_bias": _row(p["ssm_dt_bias"][l], LANE), "a_log": _row(p["ssm_a_log"][l], LANE),
        "d_skip": _row(jnp.repeat(p["ssm_d"][l], SSM_HEAD)), "norm_g": _row(p["ssm_norm_g"][l]),
        "w_gate": w_in[:, o_g:].astype(BF16), "w_br_mla": p["w_br_mla"][l].astype(BF16),
        "w_br_rwkv": p["w_br_rwkv"][l].astype(BF16), "w_br_ssm": p["w_br_ssm"][l].astype(BF16),
        "w_out": p["w_out"][l].astype(BF16), "norm_ffn_g": _row(p["norm_ffn_g"][l]),
        "w_ff1": p["w_ff1"][l].astype(BF16), "w_ff2": p["w_ff2"][l].astype(BF16),
    }
    if l > 0:
        wp["v0"] = _row(p["rwkv_v0"][l - 1])
        wp["v1"] = _pad_cols(p["rwkv_v1"][l - 1], LANE).astype(BF16)
        wp["v2"] = _pad_rows(p["rwkv_v2"][l - 1], LANE).astype(BF16)
    return wp


def kernel(x, positions, norm_mix_g, w_in, mla_q_norm_g, mla_kv_norm_g, mla_w_uq, mla_w_ukv, mla_q_head_g, mla_k_head_g, rwkv_mu, rwkv_w0, rwkv_w2, rwkv_a0, rwkv_a2, rwkv_g2, rwkv_v0, rwkv_v1, rwkv_v2, rwkv_k_k, rwkv_k_a, rwkv_r_k, rwkv_ln_g, rwkv_ln_b, ssm_conv_w, ssm_conv_b, ssm_dt_bias, ssm_a_log, ssm_d, ssm_norm_g, w_br_mla, w_br_rwkv, w_br_ssm, w_out, norm_ffn_g, w_ff1, w_ff2):
    p = dict(w_in=w_in, mla_q_norm_g=mla_q_norm_g, mla_kv_norm_g=mla_kv_norm_g, mla_w_uq=mla_w_uq,
             mla_w_ukv=mla_w_ukv, mla_q_head_g=mla_q_head_g, mla_k_head_g=mla_k_head_g, rwkv_mu=rwkv_mu,
             rwkv_w0=rwkv_w0, rwkv_w2=rwkv_w2, rwkv_a0=rwkv_a0, rwkv_a2=rwkv_a2, rwkv_g2=rwkv_g2, rwkv_v0=rwkv_v0,
             rwkv_v1=rwkv_v1, rwkv_v2=rwkv_v2, rwkv_k_k=rwkv_k_k, rwkv_k_a=rwkv_k_a, rwkv_r_k=rwkv_r_k,
             rwkv_ln_g=rwkv_ln_g, rwkv_ln_b=rwkv_ln_b, ssm_conv_w=ssm_conv_w, ssm_conv_b=ssm_conv_b,
             ssm_dt_bias=ssm_dt_bias, ssm_a_log=ssm_a_log, ssm_d=ssm_d, ssm_norm_g=ssm_norm_g, w_br_mla=w_br_mla,
             w_br_rwkv=w_br_rwkv, w_br_ssm=w_br_ssm, w_out=w_out, norm_ffn_g=norm_ffn_g, w_ff1=w_ff1, w_ff2=w_ff2)
    bsz, seq, d_model = x.shape
    depth = w_in.shape[0]
    x2 = x.reshape(bsz * seq, d_model)
    tables = _rope_tables(positions)
    v_first = None
    for l in range(depth):
        wp = _layer_weights(l, p)
        gmix = _row(norm_mix_g[l])
        o_mla = _mla_branch(x2, bsz, seq, gmix, wp, tables)
        o_rwkv, v = _rwkv_branch(x2, bsz, seq, gmix, wp, v_first)
        if l == 0:
            v_first = v
        o_ssm = _ssm_branch(x2, bsz, seq, gmix, wp)
        x2 = _merge_ffn(x2, gmix, wp, o_mla, o_rwkv, o_ssm)
    return x2.reshape(bsz, seq, d_model)
```

```python
import functools

import jax
import jax.numpy as jnp
from jax import lax
from jax.experimental import pallas as pl
from jax.experimental.pallas import tpu as pltpu

F32 = jnp.float32
BF16 = jnp.bfloat16

D_MODEL = 1024
MLA_HEADS = 8
MLA_NOPE = 64
MLA_ROPE = 32
MLA_QK = MLA_NOPE + MLA_ROPE
MLA_V = 64
MLA_Q_RANK = 384
MLA_KV_RANK = 256
ROPE_THETA = 10000.0
RWKV_HEAD = 64
RWKV_HEADS = 8
RWKV_DIM = RWKV_HEADS * RWKV_HEAD
RWKV_W_RANK = 64
RWKV_A_RANK = 64
RWKV_V_RANK = 32
RWKV_G_RANK = 128
RWKV_GN_EPS = 64e-5
SSM_HEAD = 64
SSM_HEADS = 16
SSM_DIM = SSM_HEADS * SSM_HEAD
SSM_GROUPS = 2
SSM_HPG = SSM_HEADS // SSM_GROUPS
SSM_STATE = 128
SSM_CONV = 4
SSM_CHUNK = 256
SSM_CONV_DIM = SSM_DIM + 2 * SSM_GROUPS * SSM_STATE
SSM_NORM_EPS = 1e-5
N_BRANCH = 3
D_FF = 4 * D_MODEL
NORM_EPS = 1e-6
MLA_IN = MLA_Q_RANK + MLA_KV_RANK + MLA_ROPE
RWKV_IN = 3 * RWKV_DIM + RWKV_W_RANK + RWKV_A_RANK + RWKV_G_RANK
SSM_IN = SSM_DIM + SSM_CONV_DIM + SSM_HEADS
GATE_IN = N_BRANCH * D_MODEL

LOG2_E = 1.4426950408889634
LANE = 128
SUBLANE = 8
VMEM_LIMIT = 56 * 1024 * 1024

TOK_TILE = 512
FFN_TOK_TILE = 1024
FFN_FF_TILE = 1024
ATTN_TILE = 512
ATTN_HEADS = 8
ATTN_KEYS = 256
SSM_CONV_GROUP = 512
RWKV_CHUNK = 64
RWKV_TILE = 512
RWKV_SCAN_HEADS = 8
RWKV_PACK = 4


def _dot(a, b):
    return jnp.dot(a.astype(BF16), b.astype(BF16), preferred_element_type=F32)


def _dot_nt(a, b):
    return lax.dot_general(a.astype(BF16), b.astype(BF16), (((1,), (1,)), ((), ())),
                           preferred_element_type=F32)


def _dot_tn(a, b):
    return lax.dot_general(a.astype(BF16), b.astype(BF16), (((0,), (0,)), ((), ())),
                           preferred_element_type=F32)


def _bmm(a, b):
    return lax.dot_general(a.astype(BF16), b.astype(BF16), (((2,), (1,)), ((0,), (0,))),
                           preferred_element_type=F32)


def _bmm_nt(a, b):
    return lax.dot_general(a.astype(BF16), b.astype(BF16), (((2,), (2,)), ((0,), (0,))),
                           preferred_element_type=F32)


def _bmm_tn(a, b):
    return lax.dot_general(a.astype(BF16), b.astype(BF16), (((1,), (1,)), ((0,), (0,))),
                           preferred_element_type=F32)


def _dot_exact(a, b):
    return jnp.dot(a, b, preferred_element_type=F32, precision=lax.Precision.HIGHEST)


def _dot_nt_exact(a, b):
    return lax.dot_general(a, b, (((1,), (1,)), ((), ())), preferred_element_type=F32,
                           precision=lax.Precision.HIGHEST)


def _rms(x, g, eps):
    return x * lax.rsqrt(jnp.mean(x * x, axis=-1, keepdims=True) + eps) * g


def _sigmoid(x):
    return 1.0 / (1.0 + jnp.exp(-x))


def _softplus(x):
    return jnp.maximum(x, 0.0) + jnp.log(1.0 + jnp.exp(-jnp.abs(x)))


def _iota2(shape, dim):
    return lax.broadcasted_iota(jnp.int32, shape, dim)


def _params(*sem):
    return pltpu.CompilerParams(dimension_semantics=sem, vmem_limit_bytes=VMEM_LIMIT)


def _full(shape):
    return pl.BlockSpec(shape, lambda *_: (0,) * len(shape))


def _rope_table_kernel(pos_ref, freq_ref, cos_ref, sin_ref):
    ang = pos_ref[...].astype(F32) * freq_ref[...]
    cos_ref[...] = jnp.cos(ang)
    sin_ref[...] = jnp.sin(ang)


def _rope_tables(positions):
    half = MLA_ROPE // 2
    n_tok = positions.size
    inv_freq = ROPE_THETA ** (-jnp.arange(half, dtype=jnp.float32) / half)
    per_row = LANE // half
    rows = n_tok // per_row
    pos = jnp.repeat(positions.reshape(rows, per_row), half, axis=1)
    freq = jnp.tile(inv_freq, per_row).reshape(1, LANE)
    blk = min(rows, 512)
    cos, sin = pl.pallas_call(
        _rope_table_kernel,
        grid=(rows // blk,),
        in_specs=[pl.BlockSpec((blk, LANE), lambda i: (i, 0)), _full((1, LANE))],
        out_specs=[pl.BlockSpec((blk, LANE), lambda i: (i, 0))] * 2,
        out_shape=[jax.ShapeDtypeStruct((rows, LANE), F32)] * 2,
        compiler_params=_params("parallel"),
        name="rope_table",
    )(pos, freq)
    cos = cos.reshape(n_tok, half)
    sin = sin.reshape(n_tok, half)
    one = jnp.ones((n_tok, MLA_NOPE), F32)
    z_nope = jnp.zeros((n_tok, MLA_NOPE), F32)
    z_half = jnp.zeros((n_tok, half), F32)
    z_pad = jnp.zeros((n_tok, LANE - MLA_QK), F32)
    cosf = jnp.concatenate([one, cos, cos, z_pad], axis=1)
    sina = jnp.concatenate([z_nope, -sin, z_half, z_pad], axis=1)
    sinb = jnp.concatenate([z_nope, z_half, sin, z_pad], axis=1)
    return cosf, sina, sinb


def _mla_prep_kernel(x_ref, gmix_ref, wmla_ref, qg_ref, kvg_ref, wuq_ref, wuk_ref, wuv_ref,
                     qhg_ref, khg_ref, cos_ref, sina_ref, sinb_ref, q_ref, k_ref, v_ref):
    h = _rms(x_ref[...], gmix_ref[...], NORM_EPS)
    c = _dot(h, wmla_ref[...])
    cq = _rms(c[:, :MLA_Q_RANK], qg_ref[...], NORM_EPS)
    ckv = _rms(c[:, MLA_Q_RANK:MLA_Q_RANK + MLA_KV_RANK], kvg_ref[...], NORM_EPS)
    krope = c[:, MLA_Q_RANK + MLA_KV_RANK:]
    hq = MLA_HEADS * LANE
    qall = _dot(cq, wuq_ref[...])
    kf = _dot(ckv, wuk_ref[...])
    v_ref[...] = _dot(ckv, wuv_ref[...]).astype(BF16)
    cosf, sina, sinb = cos_ref[...], sina_ref[...], sinb_ref[...]
    khg = khg_ref[...]
    scale = MLA_QK ** -0.5 * LOG2_E

    def inv_rms(t):
        return lax.rsqrt(jnp.sum(t * t, axis=-1, keepdims=True) * (1.0 / MLA_QK) + NORM_EPS)

    q_cos = qhg_ref[0:1, :] * cosf
    q_sin = qhg_ref[1:2, :] * (sinb - sina)
    k_cos = khg * cosf
    kg = krope * khg
    k_rot = pltpu.roll(kg, LANE - MLA_ROPE // 2, 1) * sina + pltpu.roll(kg, MLA_ROPE // 2, 1) * sinb

    for hd in range(MLA_HEADS):
        sl = slice(LANE * hd, LANE * (hd + 1))
        qh = qall[:, sl]
        qr = qall[:, hq + LANE * hd:hq + LANE * (hd + 1)]
        q_ref[:, sl] = ((qh * q_cos + qr * q_sin) * (inv_rms(qh) * scale)).astype(BF16)
        kh = kf[:, sl] + krope
        k_ref[:, sl] = ((kh * k_cos + k_rot) * inv_rms(kh)).astype(BF16)


def _attn_kernel(q_ref, k_ref, v_ref, o_ref, m_ref, l_ref, acc_ref):
    i = pl.program_id(2)
    tq = q_ref.shape[0]
    heads = range(ATTN_HEADS)
    m_ref[...] = jnp.full(m_ref.shape, -jnp.inf, F32)
    l_ref[...] = jnp.zeros(l_ref.shape, F32)
    acc_ref[...] = jnp.zeros(acc_ref.shape, F32)

    def step(j, masked):
        for sub in range(tq // ATTN_KEYS):
            substep(j * tq + sub * ATTN_KEYS, sub * ATTN_KEYS if masked else 0, masked)

    def substep(start, q0, masked):
        rows = pl.ds(pl.multiple_of(start, ATTN_KEYS), ATTN_KEYS)
        qs = slice(q0, tq)
        scores = [_dot_nt(k_ref[rows, LANE * hh:LANE * (hh + 1)], q_ref[qs, LANE * hh:LANE * (hh + 1)])
                  for hh in heads]
        probs = []
        for hh in heads:
            s = scores[hh]
            if masked:
                s = jnp.where(_iota2(s.shape, 1) >= _iota2(s.shape, 0), s, -jnp.inf)
            m = m_ref[hh, :, qs]
            m_new = jnp.maximum(m, jnp.max(s, axis=0, keepdims=True))
            p = jnp.exp2(s - m_new)
            alpha = jnp.exp2(m - m_new)
            m_ref[hh, :, qs] = m_new
            l_ref[hh, :, qs] = alpha * l_ref[hh, :, qs] + jnp.sum(p, axis=0, keepdims=True)
            probs.append((p.astype(BF16), alpha))
        for hh in heads:
            p, alpha = probs[hh]
            v = v_ref[rows, MLA_V * hh:MLA_V * (hh + 1)]
            acc_ref[hh, :, qs] = alpha * acc_ref[hh, :, qs] + _dot_tn(v, p)

    def body(j, carry):
        step(j, False)
        return carry

    lax.fori_loop(0, i, body, 0)
    step(i, True)
    out = jnp.concatenate([acc_ref[hh] / l_ref[hh] for hh in heads], axis=0)
    o_ref[...] = out.T.astype(BF16)


def _mla_branch(x2, bsz, seq, gmix, wp, tables):
    n_tok = x2.shape[0]
    tm = min(TOK_TILE, seq)
    cosf, sina, sinb = tables
    tok = lambda w: pl.BlockSpec((tm, w), lambda i: (i, 0))
    hq = MLA_HEADS * LANE
    q, k, v = pl.pallas_call(
        _mla_prep_kernel,
        grid=(n_tok // tm,),
        in_specs=[tok(D_MODEL), _full((1, D_MODEL)), _full(wp["w_mla"].shape), _full((1, MLA_Q_RANK)),
                  _full((1, MLA_KV_RANK)), _full(wp["w_uq"].shape), _full(wp["w_uk"].shape),
                  _full(wp["w_uv"].shape), _full((2, LANE)), _full((1, LANE)), tok(LANE), tok(LANE), tok(LANE)],
        out_specs=[tok(hq), tok(hq), tok(MLA_HEADS * MLA_V)],
        out_shape=[jax.ShapeDtypeStruct((n_tok, hq), BF16), jax.ShapeDtypeStruct((n_tok, hq), BF16),
                   jax.ShapeDtypeStruct((n_tok, MLA_HEADS * MLA_V), BF16)],
        compiler_params=_params("parallel"),
        name="mla_prep",
    )(x2, gmix, wp["w_mla"], wp["q_norm_g"], wp["kv_norm_g"], wp["w_uq"], wp["w_uk"], wp["w_uv"],
      wp["q_head_g"], wp["k_head_g"], cosf, sina, sinb)

    tq = min(ATTN_TILE, seq)
    nq = seq // tq
    return pl.pallas_call(
        _attn_kernel,
        grid=(bsz, MLA_HEADS // ATTN_HEADS, nq),
        in_specs=[pl.BlockSpec((tq, ATTN_HEADS * LANE), lambda b, hp, i: (b * nq + i, hp)),
                  pl.BlockSpec((seq, ATTN_HEADS * LANE), lambda b, hp, i: (b, hp)),
                  pl.BlockSpec((seq, ATTN_HEADS * MLA_V), lambda b, hp, i: (b, hp))],
        out_specs=pl.BlockSpec((tq, ATTN_HEADS * MLA_V), lambda b, hp, i: (b * nq + i, hp)),
        out_shape=jax.ShapeDtypeStruct((n_tok, MLA_HEADS * MLA_V), BF16),
        scratch_shapes=[pltpu.VMEM((ATTN_HEADS, 1, tq), F32), pltpu.VMEM((ATTN_HEADS, 1, tq), F32),
                        pltpu.VMEM((ATTN_HEADS, MLA_V, tq), F32)],
        compiler_params=_params("parallel", "parallel", "arbitrary"),
        name="mla_attention",
    )(q, k, v)


def _rwkv_prep_kernel(has_vres, *refs):
    if has_vres:
        (x_ref, gmix_ref, w_ref, mu_ref, w0_ref, w2_ref, a0_ref, a2_ref, g2_ref, kk_ref, ka_ref,
         vfirst_ref, v0_ref, v1_ref, v2_ref,
         r_out, lw_out, k_out, v_out, kk_out, a_out, g_out, carry_ref) = refs
    else:
        (x_ref, gmix_ref, w_ref, mu_ref, w0_ref, w2_ref, a0_ref, a2_ref, g2_ref, kk_ref, ka_ref,
         r_out, lw_out, k_out, v_out, kk_out, a_out, g_out, carry_ref) = refs

    @pl.when(pl.program_id(1) == 0)
    def _():
        carry_ref[...] = jnp.zeros_like(carry_ref)

    h = _rms(x_ref[...], gmix_ref[...], NORM_EPS).astype(BF16)
    tm = h.shape[0]
    d = RWKV_DIM

    def mixed(lo, hi):
        p = _dot(h, w_ref[:, lo:hi])
        prev = jnp.where(_iota2(p.shape, 0) == 0, carry_ref[:, lo:hi], pltpu.roll(p, 1, 0))
        carry_ref[:, lo:hi] = p[tm - 1:tm, :]
        return p + (prev - p) * mu_ref[:, lo:hi]

    lora = mixed(3 * d, RWKV_IN)
    xwa = lora[:, :LANE]
    r_out[...] = mixed(0, d)
    w = -_softplus(-(w0_ref[...] + _dot(jnp.tanh(xwa), w2_ref[...]))) - 0.5
    lw_out[...] = -jnp.exp(w)
    a = _sigmoid(a0_ref[...] + _dot(xwa, a2_ref[...]))
    a_out[...] = a
    g_out[...] = _dot(_sigmoid(lora[:, LANE:]), g2_ref[...])
    k = mixed(d, 2 * d)
    kk_out[...] = k * kk_ref[...]
    k_out[...] = k * (1.0 + (a - 1.0) * ka_ref[...])
    v = mixed(2 * d, 3 * d)
    if has_vres:
        gate = _sigmoid(v0_ref[...] + _dot(_dot(v, v1_ref[...]), v2_ref[...]))
        v = v + (vfirst_ref[...] - v) * gate
    v_out[...] = v


def _head_sum(x):
    blocks = []
    for j in range(x.shape[-1] // LANE):
        xb = x[..., LANE * j:LANE * (j + 1)]
        first = _iota2(xb.shape, xb.ndim - 1) < RWKV_HEAD
        s0 = jnp.sum(jnp.where(first, xb, 0.0), axis=-1, keepdims=True)
        s1 = jnp.sum(jnp.where(first, 0.0, xb), axis=-1, keepdims=True)
        blocks.append(jnp.where(first, s0, s1))
    return blocks[0] if len(blocks) == 1 else jnp.concatenate(blocks, axis=-1)


def _rwkv_scan_kernel(r_ref, lw_ref, k_ref, v_ref, kk_ref, a_ref, g_ref, rk_ref, lng_ref, lnb_ref,
                      o_ref, state_ref, y_ref):
    L = RWKV_CHUNK
    n = RWKV_HEAD
    ts, width = r_ref.shape
    nh = width // n
    nchunk = ts // L
    sh3 = (nchunk, L, width)

    @pl.when(pl.program_id(2) == 0)
    def _():
        state_ref[...] = jnp.zeros_like(state_ref)

    t_idx = _iota2(sh3, 1)
    r = r_ref[...].reshape(sh3)
    lw = lw_ref[...].reshape(sh3)
    k = k_ref[...].reshape(sh3)
    v = v_ref[...].reshape(sh3)
    kk = kk_ref[...].reshape(sh3)
    a = a_ref[...].reshape(sh3)

    kk = kk / jnp.maximum(jnp.sqrt(_head_sum(kk * kk)), 1e-12)
    cum = lw
    s = 1
    while s < L:
        cum = cum + jnp.where(t_idx >= s, pltpu.roll(cum, s, 1), 0.0)
        s *= 2
    tot = cum[:, L - 1:L, :]
    inv = jnp.exp(-cum)
    to_end = jnp.exp(tot - cum)

    gw = RWKV_PACK * n
    ngroup = width // gw

    def groups_to_batch(x):
        return jnp.concatenate([x[:, :, gw * j:gw * (j + 1)] for j in range(ngroup)], axis=0)

    def block_diag(x, head_lane):
        xb = x.astype(BF16)
        zero = jnp.zeros_like(xb)
        return jnp.concatenate([jnp.where(head_lane == hh, xb, zero) for hh in range(RWKV_PACK)], axis=-2)

    at = groups_to_batch(-kk * jnp.exp(cum - lw))
    rt = groups_to_batch(r * jnp.exp(cum))
    bt = groups_to_batch(kk * a * inv)
    kt = groups_to_batch(k * inv)
    bend = groups_to_batch(kk * a * to_end).astype(BF16)
    kend = groups_to_batch(k * to_end).astype(BF16)
    vb = groups_to_batch(v)
    tot2 = tot.reshape(nchunk, width)
    pad = jnp.zeros((LANE - nchunk, LANE), F32)
    gam_t = jnp.concatenate([jnp.exp(jnp.concatenate([tot2[:, LANE * j:LANE * (j + 1)], pad], axis=0).T)
                             for j in range(width // LANE)], axis=0)

    nb = ngroup * nchunk
    shift = n.bit_length() - 1
    head3 = _iota2((nb, L, gw), 2) >> shift
    bd = functools.partial(block_diag, head_lane=head3)
    row = _iota2((nb, 2 * L, gw), 1)
    col = _iota2((nb, 2 * L, gw), 2) & (n - 1)
    causal = row - col >= jnp.where(row < L, 1, L)
    eye = (_iota2((nb, L, gw), 1) == (_iota2((nb, L, gw), 2) & (n - 1))).astype(F32)
    same_head = (_iota2((nb, gw, gw), 1) >> shift) == (_iota2((nb, gw, gw), 2) >> shift)

    ar = jnp.concatenate([at, rt], axis=1)
    pb = jnp.where(causal, _bmm_nt(ar, bd(bt)), 0.0)
    pk = jnp.where(causal, _bmm_nt(ar, bd(kt)), 0.0)
    wv = _bmm(pk, bd(vb))
    mab = pb[:, :L]
    tinv = eye + mab
    pw = _bmm(mab, bd(mab))
    s = 4
    while s < L:
        both = _bmm(jnp.concatenate([tinv, pw], axis=1), bd(pw))
        tinv = tinv + both[:, :L]
        pw = both[:, L:]
        s *= 2
    tinv = tinv + _bmm(tinv, bd(pw))
    abar = _bmm(tinv, bd(at))
    u0 = _bmm(tinv, bd(wv[:, :L]))
    wmat = jnp.where(same_head, _bmm_tn(bend, abar), 0.0).astype(BF16)
    nmat = jnp.where(same_head, _bmm_tn(jnp.concatenate([bend, kend], axis=1),
                                        jnp.concatenate([u0, vb], axis=1)), 0.0)
    ar = jnp.concatenate([abar, rt], axis=1).astype(BF16)
    nrb = pb[:, L:].astype(BF16)
    y0 = wv[:, L:]

    gs = range(ngroup)
    head2 = _iota2((L, gw), 1) >> shift
    st = [state_ref[j] for j in gs]
    for c in range(nchunk):
        idx = [j * nchunk + c for j in gs]
        upd = [_dot(wmat[idx[j]], st[j]) for j in gs]
        rd = [_dot(ar[idx[j]], st[j]) for j in gs]
        for j in gs:
            u = rd[j][:L] + u0[idx[j]]
            y_ref[L * c:L * (c + 1), gw * j:gw * (j + 1)] = (
                rd[j][L:] + _dot(nrb[idx[j]], block_diag(u, head2)) + y0[idx[j]])
            st[j] = st[j] * gam_t[gw * j:gw * (j + 1), c:c + 1] + upd[j] + nmat[idx[j]]
    for j in gs:
        state_ref[j] = st[j]

    y = y_ref[...]
    mean = _head_sum(y) * (1.0 / n)
    var = _head_sum(jnp.square(y - mean)) * (1.0 / n)
    y = (y - mean) * lax.rsqrt(var + RWKV_GN_EPS)
    y = y * lng_ref[...] + lnb_ref[...]
    y = y + _head_sum(r_ref[...] * k_ref[...] * rk_ref[...]) * v_ref[...]
    o_ref[...] = (y * g_ref[...]).astype(BF16)


def _rwkv_branch(x2, bsz, seq, gmix, wp, v_first):
    n_tok = x2.shape[0]
    tm = min(TOK_TILE, seq)
    ns = seq // tm
    has_vres = v_first is not None
    d = RWKV_DIM
    tok = lambda w: pl.BlockSpec((tm, w), lambda b, i: (b * ns + i, 0))
    vec = _full((1, d))
    in_specs = [tok(D_MODEL), _full((1, D_MODEL)), _full((D_MODEL, RWKV_IN)), _full((1, RWKV_IN)), vec,
                _full((LANE, d)), vec, _full((LANE, d)), _full((RWKV_G_RANK, d)), vec, vec]
    args = [x2, gmix, wp["w_rwkv"], wp["mu"], wp["w0"], wp["w2"], wp["a0"], wp["a2"], wp["g2"], wp["k_k"], wp["k_a"]]
    if has_vres:
        in_specs += [tok(d), vec, _full((d, LANE)), _full((LANE, d))]
        args += [v_first, wp["v0"], wp["v1"], wp["v2"]]
    r, lw, k, v, kk, a, g = pl.pallas_call(
        functools.partial(_rwkv_prep_kernel, has_vres),
        grid=(bsz, ns),
        in_specs=in_specs,
        out_specs=[tok(d)] * 7,
        out_shape=[jax.ShapeDtypeStruct((n_tok, d), F32)] * 7,
        scratch_shapes=[pltpu.VMEM((1, RWKV_IN), F32)],
        compiler_params=_params("parallel", "arbitrary"),
        name="rwkv_prep",
    )(*args)

    ts = min(RWKV_TILE, seq)
    nt = seq // ts
    width = RWKV_SCAN_HEADS * RWKV_HEAD
    blk = pl.BlockSpec((ts, width), lambda b, hp, i: (b * nt + i, hp))
    par = pl.BlockSpec((1, width), lambda b, hp, i: (0, hp))
    o = pl.pallas_call(
        _rwkv_scan_kernel,
        grid=(bsz, RWKV_HEADS // RWKV_SCAN_HEADS, nt),
        in_specs=[blk] * 7 + [par] * 3,
        out_specs=blk,
        out_shape=jax.ShapeDtypeStruct((n_tok, d), BF16),
        scratch_shapes=[pltpu.VMEM((RWKV_SCAN_HEADS // RWKV_PACK, RWKV_PACK * RWKV_HEAD, RWKV_PACK * RWKV_HEAD), F32),
                        pltpu.VMEM((ts, width), F32)],
        compiler_params=_params("parallel", "parallel", "arbitrary"),
        name="rwkv_scan",
    )(r, lw, k, v, kk, a, g, wp["r_k"], wp["ln_g"], wp["ln_b"])
    return o, v


def _ssm_prep_kernel(x_ref, gmix_ref, w_ref, cw_ref, cb_ref, dtb_ref,
                     z_out, xs_out, b_out, c_out, dt_out, carry_ref):
    @pl.when(pl.program_id(1) == 0)
    def _():
        carry_ref[...] = jnp.zeros_like(carry_ref)

    h = _rms(x_ref[...], gmix_ref[...], NORM_EPS).astype(BF16)
    tm = h.shape[0]
    gw = SSM_CONV_GROUP
    top_row = _iota2((SUBLANE, gw), 0)
    for j in range(SSM_CONV_DIM // gw):
        cs = slice(gw * j, gw * (j + 1))
        xbc = _dot(h, w_ref[:, SSM_DIM + gw * j:SSM_DIM + gw * (j + 1)])
        if gw * j < SSM_DIM:
            z_out[:, cs] = _dot(h, w_ref[:, cs])
        else:
            dt_out[...] = _softplus(_dot(h, w_ref[:, SSM_DIM + SSM_CONV_DIM:]) + dtb_ref[...])
        prev = carry_ref[:, cs]
        carry_ref[:, cs] = xbc[tm - SUBLANE:, :]
        acc = xbc * cw_ref[SSM_CONV - 1:SSM_CONV, cs] + cb_ref[:, cs]
        for s in range(1, SSM_CONV):
            rolled = pltpu.roll(xbc, s, 0)
            top = jnp.where(top_row < s, pltpu.roll(prev, s, 0), rolled[:SUBLANE])
            shifted = jnp.concatenate([top, rolled[SUBLANE:]], axis=0)
            acc = acc + shifted * cw_ref[SSM_CONV - 1 - s:SSM_CONV - s, cs]
        act = acc * _sigmoid(acc)
        if gw * j < SSM_DIM:
            xs_out[:, cs] = act
        else:
            gn = SSM_GROUPS * SSM_STATE
            b_out[...] = act[:, :gn].astype(BF16)
            c_out[...] = act[:, gn:].astype(BF16)


def _ssd_kernel(xs_ref, b_ref, c_ref, dt_ref, z_ref, alog_ref, d_ref, ng_ref, o_ref, state_ref, y_ref):
    @pl.when(pl.program_id(1) == 0)
    def _():
        state_ref[...] = jnp.zeros_like(state_ref)

    L = xs_ref.shape[0]
    row = _iota2((L, L), 0)
    col = _iota2((L, L), 1)
    lower = row >= col
    tril = lower.astype(F32)
    eye = (_iota2((LANE, LANE), 0) == _iota2((LANE, LANE), 1)).astype(F32)

    dt = dt_ref[...]
    da = dt * (-jnp.exp(alog_ref[...]))
    acs = _dot_exact(tril, da)
    acs_t = _dot_nt_exact(eye, acs)
    dt_t = _dot_nt_exact(eye, dt)
    first = _iota2((1, LANE), 1) < SSM_HEAD

    for g in range(SSM_GROUPS):
        gs = slice(SSM_STATE * g, SSM_STATE * (g + 1))
        bm = b_ref[:, gs]
        cm = c_ref[:, gs]
        cb = _dot_nt(cm, bm)
        bm_t = _dot_nt(eye, bm)
        for e in range(0, SSM_HPG, 2):
            pair = (SSM_HPG * g + e) // 2
            ps = slice(LANE * pair, LANE * (pair + 1))
            xs = xs_ref[:, ps]
            xs16 = xs.astype(BF16)
            st = state_ref[pair]
            y_in = _dot(cm, st)
            ys, sts = [], []
            for hd in (2 * pair, 2 * pair + 1):
                a_col = jnp.broadcast_to(acs[:, hd:hd + 1], (L, LANE))
                a_row = acs_t[hd:hd + 1, :]
                dt_row = dt_t[hd:hd + 1, :]
                tot = acs[L - 1:L, hd:hd + 1]
                seg = jnp.concatenate([jnp.exp(a_col - a_row[:, LANE * j:LANE * (j + 1)])
                                       for j in range(L // LANE)], axis=1)
                w = cb * jnp.where(lower, seg, 0.0) * dt_row
                ys.append(_dot(w, xs16) + y_in * jnp.exp(a_col))
                sts.append(st * jnp.exp(tot) + _dot(bm_t * (dt_row * jnp.exp(tot - a_row)), xs16))
            y_ref[:, ps] = jnp.where(first, ys[0], ys[1]) + xs * d_ref[:, ps]
            state_ref[pair] = jnp.where(first, sts[0], sts[1])

    z = z_ref[...]
    y = y_ref[...] * (z * _sigmoid(z))
    gw = SSM_DIM // SSM_GROUPS
    for g in range(SSM_GROUPS):
        gs = slice(gw * g, gw * (g + 1))
        yg = y[:, gs]
        yg = yg * lax.rsqrt(jnp.mean(yg * yg, axis=-1, keepdims=True) + SSM_NORM_EPS)
        o_ref[:, gs] = (yg * ng_ref[:, gs]).astype(BF16)


def _ssm_branch(x2, bsz, seq, gmix, wp):
    n_tok = x2.shape[0]
    tm = min(TOK_TILE, seq)
    ns = seq // tm
    gn = SSM_GROUPS * SSM_STATE
    tok = lambda w: pl.BlockSpec((tm, w), lambda b, i: (b * ns + i, 0))
    z, xs, bm, cm, dt = pl.pallas_call(
        _ssm_prep_kernel,
        grid=(bsz, ns),
        in_specs=[tok(D_MODEL), _full((1, D_MODEL)), _full(wp["w_ssm"].shape), _full((SSM_CONV, SSM_CONV_DIM)),
                  _full((1, SSM_CONV_DIM)), _full((1, LANE))],
        out_specs=[tok(SSM_DIM), tok(SSM_DIM), tok(gn), tok(gn), tok(LANE)],
        out_shape=[jax.ShapeDtypeStruct((n_tok, SSM_DIM), F32), jax.ShapeDtypeStruct((n_tok, SSM_DIM), F32),
                   jax.ShapeDtypeStruct((n_tok, gn), BF16), jax.ShapeDtypeStruct((n_tok, gn), BF16),
                   jax.ShapeDtypeStruct((n_tok, LANE), F32)],
        scratch_shapes=[pltpu.VMEM((SUBLANE, SSM_CONV_DIM), F32)],
        compiler_params=_params("parallel", "arbitrary"),
        name="ssm_prep",
    )(x2, gmix, wp["w_ssm"], wp["conv_w"], wp["conv_b"], wp["dt_bias"])

    assert seq % SSM_CHUNK == 0
    nc = seq // SSM_CHUNK
    ch = lambda w: pl.BlockSpec((SSM_CHUNK, w), lambda b, c: (b * nc + c, 0))
    return pl.pallas_call(
        _ssd_kernel,
        grid=(bsz, nc),
        in_specs=[ch(SSM_DIM), ch(gn), ch(gn), ch(LANE), ch(SSM_DIM), _full((1, LANE)), _full((1, SSM_DIM)),
                  _full((1, SSM_DIM))],
        out_specs=ch(SSM_DIM),
        out_shape=jax.ShapeDtypeStruct((n_tok, SSM_DIM), BF16),
        scratch_shapes=[pltpu.VMEM((SSM_HEADS // 2, SSM_STATE, 2 * SSM_HEAD), F32),
                        pltpu.VMEM((SSM_CHUNK, SSM_DIM), F32)],
        compiler_params=_params("parallel", "arbitrary"),
        name="ssd",
    )(xs, bm, cm, dt, z, wp["a_log"], wp["d_skip"], wp["norm_g"])


def _merge_kernel(x_ref, gmix_ref, wg_ref, om_ref, or_ref, os_ref, wm_ref, wr_ref, ws_ref, wo_ref, o_ref):
    x = x_ref[...]
    h = _rms(x, gmix_ref[...], NORM_EPS).astype(BF16)
    d = D_MODEL
    merged = _sigmoid(_dot(h, wg_ref[:, :d])) * _dot(om_ref[...], wm_ref[...])
    merged = merged + _sigmoid(_dot(h, wg_ref[:, d:2 * d])) * _dot(or_ref[...], wr_ref[...])
    merged = merged + _sigmoid(_dot(h, wg_ref[:, 2 * d:])) * _dot(os_ref[...], ws_ref[...])
    o_ref[...] = x + _dot(merged, wo_ref[...])


def _ffn_kernel(x_ref, g_ref, w1_ref, w2_ref, o_ref, h_ref):
    j = pl.program_id(1)

    @pl.when(j == 0)
    def _():
        x = x_ref[...]
        h_ref[...] = _rms(x, g_ref[...], NORM_EPS).astype(BF16)
        o_ref[...] = x

    u = jnp.maximum(_dot(h_ref[...], w1_ref[...]), 0.0)
    o_ref[...] += _dot(u * u, w2_ref[...])


def _merge_ffn(x2, gmix, wp, o_mla, o_rwkv, o_ssm):
    n_tok = x2.shape[0]
    tm = min(TOK_TILE, n_tok)
    tok = lambda w: pl.BlockSpec((tm, w), lambda i: (i, 0))
    x2 = pl.pallas_call(
        _merge_kernel,
        grid=(n_tok // tm,),
        in_specs=[tok(D_MODEL), _full((1, D_MODEL)), _full((D_MODEL, GATE_IN)), tok(o_mla.shape[1]),
                  tok(o_rwkv.shape[1]), tok(o_ssm.shape[1]), _full(wp["w_br_mla"].shape),
                  _full(wp["w_br_rwkv"].shape), _full(wp["w_br_ssm"].shape), _full((D_MODEL, D_MODEL))],
        out_specs=tok(D_MODEL),
        out_shape=jax.ShapeDtypeStruct((n_tok, D_MODEL), F32),
        compiler_params=_params("parallel"),
        name="merge",
    )(x2, gmix, wp["w_gate"], o_mla, o_rwkv, o_ssm, wp["w_br_mla"], wp["w_br_rwkv"], wp["w_br_ssm"], wp["w_out"])

    tf = min(FFN_TOK_TILE, n_tok)
    ff = FFN_FF_TILE
    return pl.pallas_call(
        _ffn_kernel,
        grid=(n_tok // tf, D_FF // ff),
        in_specs=[pl.BlockSpec((tf, D_MODEL), lambda i, j: (i, 0)), _full((1, D_MODEL)),
                  pl.BlockSpec((D_MODEL, ff), lambda i, j: (0, j)), pl.BlockSpec((ff, D_MODEL), lambda i, j: (j, 0))],
        out_specs=pl.BlockSpec((tf, D_MODEL), lambda i, j: (i, 0)),
        out_shape=jax.ShapeDtypeStruct((n_tok, D_MODEL), F32),
        scratch_shapes=[pltpu.VMEM((tf, D_MODEL), BF16)],
        compiler_params=_params("parallel", "arbitrary"),
        name="ffn",
    )(x2, wp["norm_ffn_g"], wp["w_ff1"], wp["w_ff2"])


def _pad_cols(w, width):
    return jnp.pad(w, ((0, 0), (0, width - w.shape[1])))


def _pad_rows(w, height, before=0):
    return jnp.pad(w, ((before, height - before - w.shape[0]), (0, 0)))


def _row(v, width=None):
    v = v.reshape(1, -1).astype(F32)
    return v if width is None else _pad_cols(v, width)


def _layer_weights(l, p):
    w_in = p["w_in"][l]
    o_r = MLA_IN
    o_s = o_r + RWKV_IN
    o_g = o_s + SSM_IN
    zc = lambda n: jnp.zeros((D_MODEL, n), w_in.dtype)
    w_mla = jnp.concatenate([w_in[:, :MLA_Q_RANK + MLA_KV_RANK], zc(MLA_NOPE), w_in[:, MLA_Q_RANK + MLA_KV_RANK:MLA_IN],
                             zc(LANE - MLA_QK)], axis=1)
    half = MLA_ROPE // 2

    def slot(t, rotate):
        lo, hi = t[..., MLA_NOPE:MLA_NOPE + half], t[..., MLA_NOPE + half:]
        parts = [jnp.zeros_like(t[..., :MLA_NOPE]), -hi if rotate == "signed" else hi, lo] if rotate else [t]
        t = jnp.concatenate(parts, axis=-1)
        return jnp.pad(t, [(0, 0)] * (t.ndim - 1) + [(0, LANE - MLA_QK)])

    w_uq = p["mla_w_uq"][l].reshape(MLA_Q_RANK, MLA_HEADS, MLA_QK)
    w_uq = jnp.concatenate([slot(w_uq, None).reshape(MLA_Q_RANK, MLA_HEADS * LANE),
                            slot(w_uq, "signed").reshape(MLA_Q_RANK, MLA_HEADS * LANE)], axis=1)
    q_head_g = p["mla_q_head_g"][l].astype(F32)
    q_head_g = jnp.stack([slot(q_head_g, None), slot(q_head_g, "moved")])
    w_ukv = p["mla_w_ukv"][l].reshape(MLA_KV_RANK, MLA_HEADS, MLA_NOPE + MLA_V)
    w_uk = jnp.pad(w_ukv[:, :, :MLA_NOPE], ((0, 0), (0, 0), (0, LANE - MLA_NOPE))).reshape(MLA_KV_RANK, MLA_HEADS * LANE)
    w_uv = w_ukv[:, :, MLA_NOPE:].reshape(MLA_KV_RANK, MLA_HEADS * MLA_V)
    w_ssm = _pad_cols(w_in[:, o_s:o_g], SSM_DIM + SSM_CONV_DIM + LANE)
    wp = {
        "w_mla": w_mla.astype(BF16), "w_uq": w_uq.astype(BF16), "w_uk": w_uk.astype(BF16), "w_uv": w_uv.astype(BF16),
        "q_norm_g": _row(p["mla_q_norm_g"][l]), "kv_norm_g": _row(p["mla_kv_norm_g"][l]),
        "q_head_g": q_head_g, "k_head_g": _row(p["mla_k_head_g"][l], LANE),
        "w_rwkv": w_in[:, o_r:o_s].astype(BF16), "mu": _row(p["rwkv_mu"][l]), "w0": _row(p["rwkv_w0"][l]),
        "w2": _pad_rows(p["rwkv_w2"][l], LANE).astype(BF16), "a0": _row(p["rwkv_a0"][l]),
        "a2": _pad_rows(p["rwkv_a2"][l], LANE, before=RWKV_W_RANK).astype(BF16), "g2": p["rwkv_g2"][l].astype(BF16),
        "k_k": _row(p["rwkv_k_k"][l]), "k_a": _row(p["rwkv_k_a"][l]), "r_k": _row(p["rwkv_r_k"][l]),
        "ln_g": _row(p["rwkv_ln_g"][l]), "ln_b": _row(p["rwkv_ln_b"][l]),
        "w_ssm": w_ssm.astype(BF16), "conv_w": p["ssm_conv_w"][l].astype(F32), "conv_b": _row(p["ssm_conv_b"][l]),
        "dt_bias": _row(p["ssm_dt_bias"][l], LANE), "a_log": _row(p["ssm_a_log"][l], LANE),
        "d_skip": _row(jnp.repeat(p["ssm_d"][l], SSM_HEAD)), "norm_g": _row(p["ssm_norm_g"][l]),
        "w_gate": w_in[:, o_g:].astype(BF16), "w_br_mla": p["w_br_mla"][l].astype(BF16),
        "w_br_rwkv": p["w_br_rwkv"][l].astype(BF16), "w_br_ssm": p["w_br_ssm"][l].astype(BF16),
        "w_out": p["w_out"][l].astype(BF16), "norm_ffn_g": _row(p["norm_ffn_g"][l]),
        "w_ff1": p["w_ff1"][l].astype(BF16), "w_ff2": p["w_ff2"][l].astype(BF16),
    }
    if l > 0:
        wp["v0"] = _row(p["rwkv_v0"][l - 1])
        wp["v1"] = _pad_cols(p["rwkv_v1"][l - 1], LANE).astype(BF16)
        wp["v2"] = _pad_rows(p["rwkv_v2"][l - 1], LANE).astype(BF16)
    return wp


def kernel(x, positions, norm_mix_g, w_in, mla_q_norm_g, mla_kv_norm_g, mla_w_uq, mla_w_ukv, mla_q_head_g, mla_k_head_g, rwkv_mu, rwkv_w0, rwkv_w2, rwkv_a0, rwkv_a2, rwkv_g2, rwkv_v0, rwkv_v1, rwkv_v2, rwkv_k_k, rwkv_k_a, rwkv_r_k, rwkv_ln_g, rwkv_ln_b, ssm_conv_w, ssm_conv_b, ssm_dt_bias, ssm_a_log, ssm_d, ssm_norm_g, w_br_mla, w_br_rwkv, w_br_ssm, w_out, norm_ffn_g, w_ff1, w_ff2):
    p = dict(w_in=w_in, mla_q_norm_g=mla_q_norm_g, mla_kv_norm_g=mla_kv_norm_g, mla_w_uq=mla_w_uq,
             mla_w_ukv=mla_w_ukv, mla_q_head_g=mla_q_head_g, mla_k_head_g=mla_k_head_g, rwkv_mu=rwkv_mu,
             rwkv_w0=rwkv_w0, rwkv_w2=rwkv_w2, rwkv_a0=rwkv_a0, rwkv_a2=rwkv_a2, rwkv_g2=rwkv_g2, rwkv_v0=rwkv_v0,
             rwkv_v1=rwkv_v1, rwkv_v2=rwkv_v2, rwkv_k_k=rwkv_k_k, rwkv_k_a=rwkv_k_a, rwkv_r_k=rwkv_r_k,
             rwkv_ln_g=rwkv_ln_g, rwkv_ln_b=rwkv_ln_b, ssm_conv_w=ssm_conv_w, ssm_conv_b=ssm_conv_b,
             ssm_dt_bias=ssm_dt_bias, ssm_a_log=ssm_a_log, ssm_d=ssm_d, ssm_norm_g=ssm_norm_g, w_br_mla=w_br_mla,
             w_br_rwkv=w_br_rwkv, w_br_ssm=w_br_ssm, w_out=w_out, norm_ffn_g=norm_ffn_g, w_ff1=w_ff1, w_ff2=w_ff2)
    bsz, seq, d_model = x.shape
    depth = w_in.shape[0]
    x2 = x.reshape(bsz * seq, d_model)
    tables = _rope_tables(positions)
    v_first = None
    for l in range(depth):
        wp = _layer_weights(l, p)
        gmix = _row(norm_mix_g[l])
        o_mla = _mla_branch(x2, bsz, seq, gmix, wp, tables)
        o_rwkv, v = _rwkv_branch(x2, bsz, seq, gmix, wp, v_first)
        if l == 0:
            v_first = v
        o_ssm = _ssm_branch(x2, bsz, seq, gmix, wp)
        x2 = _merge_ffn(x2, gmix, wp, o_mla, o_rwkv, o_ssm)
    return x2.reshape(bsz, seq, d_model)
```

```python
import functools

import jax
import jax.numpy as jnp
from jax import lax
from jax.experimental import pallas as pl
from jax.experimental.pallas import tpu as pltpu

F32 = jnp.float32
BF16 = jnp.bfloat16

D_MODEL = 1024
MLA_HEADS = 8
MLA_NOPE = 64
MLA_ROPE = 32
MLA_QK = MLA_NOPE + MLA_ROPE
MLA_V = 64
MLA_Q_RANK = 384
MLA_KV_RANK = 256
ROPE_THETA = 10000.0
RWKV_HEAD = 64
RWKV_HEADS = 8
RWKV_DIM = RWKV_HEADS * RWKV_HEAD
RWKV_W_RANK = 64
RWKV_A_RANK = 64
RWKV_V_RANK = 32
RWKV_G_RANK = 128
RWKV_GN_EPS = 64e-5
SSM_HEAD = 64
SSM_HEADS = 16
SSM_DIM = SSM_HEADS * SSM_HEAD
SSM_GROUPS = 2
SSM_HPG = SSM_HEADS // SSM_GROUPS
SSM_STATE = 128
SSM_CONV = 4
SSM_CHUNK = 256
SSM_CONV_DIM = SSM_DIM + 2 * SSM_GROUPS * SSM_STATE
SSM_NORM_EPS = 1e-5
N_BRANCH = 3
D_FF = 4 * D_MODEL
NORM_EPS = 1e-6
MLA_IN = MLA_Q_RANK + MLA_KV_RANK + MLA_ROPE
RWKV_IN = 3 * RWKV_DIM + RWKV_W_RANK + RWKV_A_RANK + RWKV_G_RANK
SSM_IN = SSM_DIM + SSM_CONV_DIM + SSM_HEADS
GATE_IN = N_BRANCH * D_MODEL

LOG2_E = 1.4426950408889634
LANE = 128
SUBLANE = 8
VMEM_LIMIT = 56 * 1024 * 1024

TOK_TILE = 512
MLA_PREP_PARTS = 2
FFN_TOK_TILE = 1024
FFN_FF_TILE = 1024
ATTN_TILE = 512
ATTN_HEADS = 8
ATTN_KEYS = 256
SSM_CONV_GROUP = 512
RWKV_CHUNK = 64
RWKV_TILE = 512
RWKV_SCAN_HEADS = 8
RWKV_PACK = 4


def _dot(a, b):
    return jnp.dot(a.astype(BF16), b.astype(BF16), preferred_element_type=F32)


def _dot_nt(a, b):
    return lax.dot_general(a.astype(BF16), b.astype(BF16), (((1,), (1,)), ((), ())),
                           preferred_element_type=F32)


def _dot_tn(a, b):
    return lax.dot_general(a.astype(BF16), b.astype(BF16), (((0,), (0,)), ((), ())),
                           preferred_element_type=F32)


def _bmm(a, b):
    return lax.dot_general(a.astype(BF16), b.astype(BF16), (((2,), (1,)), ((0,), (0,))),
                           preferred_element_type=F32)


def _bmm_nt(a, b):
    return lax.dot_general(a.astype(BF16), b.astype(BF16), (((2,), (2,)), ((0,), (0,))),
                           preferred_element_type=F32)


def _bmm_tn(a, b):
    return lax.dot_general(a.astype(BF16), b.astype(BF16), (((1,), (1,)), ((0,), (0,))),
                           preferred_element_type=F32)


def _dot_exact(a, b):
    return jnp.dot(a, b, preferred_element_type=F32, precision=lax.Precision.HIGHEST)


def _dot_nt_exact(a, b):
    return lax.dot_general(a, b, (((1,), (1,)), ((), ())), preferred_element_type=F32,
                           precision=lax.Precision.HIGHEST)


def _rms(x, g, eps):
    return x * lax.rsqrt(jnp.mean(x * x, axis=-1, keepdims=True) + eps) * g


def _sigmoid(x):
    return 1.0 / (1.0 + jnp.exp(-x))


def _softplus(x):
    return jnp.maximum(x, 0.0) + jnp.log(1.0 + jnp.exp(-jnp.abs(x)))


def _iota2(shape, dim):
    return lax.broadcasted_iota(jnp.int32, shape, dim)


def _params(*sem):
    return pltpu.CompilerParams(dimension_semantics=sem, vmem_limit_bytes=VMEM_LIMIT)


def _full(shape):
    return pl.BlockSpec(shape, lambda *_: (0,) * len(shape))


def _rope_table_kernel(pos_ref, freq_ref, cos_ref, sin_ref):
    ang = pos_ref[...].astype(F32) * freq_ref[...]
    cos_ref[...] = jnp.cos(ang)
    sin_ref[...] = jnp.sin(ang)


def _rope_tables(positions):
    half = MLA_ROPE // 2
    n_tok = positions.size
    inv_freq = ROPE_THETA ** (-jnp.arange(half, dtype=jnp.float32) / half)
    per_row = LANE // half
    rows = n_tok // per_row
    pos = jnp.broadcast_to(positions.reshape(rows, per_row, 1), (rows, per_row, half)).reshape(rows, LANE)
    freq = jnp.tile(inv_freq, per_row).reshape(1, LANE)
    blk = min(rows, 512)
    cos, sin = pl.pallas_call(
        _rope_table_kernel,
        grid=(rows // blk,),
        in_specs=[pl.BlockSpec((blk, LANE), lambda i: (i, 0)), _full((1, LANE))],
        out_specs=[pl.BlockSpec((blk, LANE), lambda i: (i, 0))] * 2,
        out_shape=[jax.ShapeDtypeStruct((rows, LANE), F32)] * 2,
        compiler_params=_params("parallel"),
        name="rope_table",
    )(pos, freq)
    cos = cos.reshape(n_tok, half)
    sin = sin.reshape(n_tok, half)
    one = jnp.ones((n_tok, MLA_NOPE), F32)
    z_nope = jnp.zeros((n_tok, MLA_NOPE), F32)
    z_half = jnp.zeros((n_tok, half), F32)
    z_pad = jnp.zeros((n_tok, LANE - MLA_QK), F32)
    cosf = jnp.concatenate([one, cos, cos, z_pad], axis=1)
    sina = jnp.concatenate([z_nope, -sin, z_half, z_pad], axis=1)
    sinb = jnp.concatenate([z_nope, z_half, sin, z_pad], axis=1)
    return cosf, sina, sinb


def _mla_prep_kernel(x_ref, gmix_ref, wmla_ref, qg_ref, kvg_ref, wuq_ref, wuk_ref, wuv_ref,
                     qhg_ref, khg_ref, cos_ref, sina_ref, sinb_ref, q_ref, k_ref, v_ref):
    hq = MLA_HEADS * LANE
    tm = x_ref.shape[0]
    parts = [slice(tm * i // MLA_PREP_PARTS, tm * (i + 1) // MLA_PREP_PARTS) for i in range(MLA_PREP_PARTS)]
    one_lane = (_iota2((1, hq), 1) & (LANE - 1)) == MLA_V
    khg = khg_ref[...]
    scale = MLA_QK ** -0.5 * LOG2_E

    def inv_rms(t):
        return lax.rsqrt(jnp.sum(t * t, axis=-1, keepdims=True) * (1.0 / MLA_QK) + NORM_EPS)

    latent = [_dot(_rms(x_ref[rs, :], gmix_ref[...], NORM_EPS), wmla_ref[...])
              for rs in parts]
    up = []
    for rs, c in zip(parts, latent):
        cq = _rms(c[:, :MLA_Q_RANK], qg_ref[...], NORM_EPS)
        ckv = _rms(c[:, MLA_Q_RANK:MLA_Q_RANK + MLA_KV_RANK], kvg_ref[...], NORM_EPS)
        qall = _dot(cq, wuq_ref[...])
        kf = _dot(ckv, wuk_ref[...])
        v_ref[rs, :] = jnp.where(one_lane, 1.0, _dot(ckv, wuv_ref[...])).astype(BF16)
        up.append((qall, kf))
    for rs, c, (qall, kf) in zip(parts, latent, up):
        krope = c[:, MLA_Q_RANK + MLA_KV_RANK:]
        cosf, sina, sinb = cos_ref[rs, :], sina_ref[rs, :], sinb_ref[rs, :]
        q_cos = qhg_ref[0:1, :] * cosf
        q_sin = qhg_ref[1:2, :] * (sinb - sina)
        k_cos = khg * cosf
        kg = krope * khg
        k_rot = pltpu.roll(kg, LANE - MLA_ROPE // 2, 1) * sina + pltpu.roll(kg, MLA_ROPE // 2, 1) * sinb
        for hd in range(MLA_HEADS):
            sl = slice(LANE * hd, LANE * (hd + 1))
            qh = qall[:, sl]
            qr = qall[:, hq + LANE * hd:hq + LANE * (hd + 1)]
            q_ref[rs, sl] = ((qh * q_cos + qr * q_sin) * (inv_rms(qh) * scale)).astype(BF16)
            kh = kf[:, sl] + krope
            k_ref[rs, sl] = ((kh * k_cos + k_rot) * inv_rms(kh)).astype(BF16)


def _attn_kernel(q_ref, k_ref, v_ref, o_ref, m_ref, acc_ref):
    i = pl.program_id(2)
    tq = q_ref.shape[0]
    heads = range(ATTN_HEADS)
    acc_rows = acc_ref.shape[1]
    m_ref[...] = jnp.full(m_ref.shape, -jnp.inf, F32)
    acc_ref[...] = jnp.zeros(acc_ref.shape, F32)

    def step(j, masked):
        for sub in range(tq // ATTN_KEYS):
            substep(j * tq + sub * ATTN_KEYS, sub * ATTN_KEYS if masked else 0, masked)

    def substep(start, q0, masked):
        rows = pl.ds(pl.multiple_of(start, ATTN_KEYS), ATTN_KEYS)
        qs = slice(q0, tq)
        scores = [_dot_nt(k_ref[rows, LANE * hh:LANE * (hh + 1)], q_ref[qs, LANE * hh:LANE * (hh + 1)])
                  for hh in heads]
        probs = []
        for hh in heads:
            s = scores[hh]
            if masked:
                s = jnp.where(_iota2(s.shape, 1) >= _iota2(s.shape, 0), s, -jnp.inf)
            m = m_ref[hh, :, qs]
            m_new = jnp.maximum(m, jnp.max(s, axis=0, keepdims=True))
            p = jnp.exp2(s - m_new)
            alpha = jnp.exp2(m - m_new)
            m_ref[hh, :, qs] = m_new
            probs.append((p.astype(BF16), alpha))
        for hh in heads:
            p, alpha = probs[hh]
            v = v_ref[rows, LANE * hh:LANE * (hh + 1)]
            acc_ref[hh, :, qs] = alpha * acc_ref[hh, :, qs] + _dot_tn(v, p)[:acc_rows]

    def body(j, carry):
        step(j, False)
        return carry

    lax.fori_loop(0, i, body, 0)
    step(i, True)
    out = jnp.concatenate([acc_ref[hh, :MLA_V, :] / acc_ref[hh, MLA_V:MLA_V + 1, :] for hh in heads],
                          axis=0)
    o_ref[...] = out.T.astype(BF16)


def _mla_branch(x2, bsz, seq, gmix, wp, tables):
    n_tok = x2.shape[0]
    tm = min(TOK_TILE, seq)
    cosf, sina, sinb = tables
    tok = lambda w: pl.BlockSpec((tm, w), lambda i: (i, 0))
    hq = MLA_HEADS * LANE
    q, k, v = pl.pallas_call(
        _mla_prep_kernel,
        grid=(n_tok // tm,),
        in_specs=[tok(D_MODEL), _full((1, D_MODEL)), _full(wp["w_mla"].shape), _full((1, MLA_Q_RANK)),
                  _full((1, MLA_KV_RANK)), _full(wp["w_uq"].shape), _full(wp["w_uk"].shape),
                  _full(wp["w_uv"].shape), _full((2, LANE)), _full((1, LANE)), tok(LANE), tok(LANE), tok(LANE)],
        out_specs=[tok(hq)] * 3,
        out_shape=[jax.ShapeDtypeStruct((n_tok, hq), BF16)] * 3,
        compiler_params=_params("parallel"),
        name="mla_prep",
    )(x2, gmix, wp["w_mla"], wp["q_norm_g"], wp["kv_norm_g"], wp["w_uq"], wp["w_uk"], wp["w_uv"],
      wp["q_head_g"], wp["k_head_g"], cosf, sina, sinb)

    tq = min(ATTN_TILE, seq)
    nq = seq // tq
    return pl.pallas_call(
        _attn_kernel,
        grid=(bsz, MLA_HEADS // ATTN_HEADS, nq),
        in_specs=[pl.BlockSpec((tq, ATTN_HEADS * LANE), lambda b, hp, i: (b * nq + i, hp)),
                  pl.BlockSpec((seq, ATTN_HEADS * LANE), lambda b, hp, i: (b, hp)),
                  pl.BlockSpec((seq, ATTN_HEADS * LANE), lambda b, hp, i: (b, hp))],
        out_specs=pl.BlockSpec((tq, ATTN_HEADS * MLA_V), lambda b, hp, i: (b * nq + i, hp)),
        out_shape=jax.ShapeDtypeStruct((n_tok, MLA_HEADS * MLA_V), BF16),
        scratch_shapes=[pltpu.VMEM((ATTN_HEADS, 1, tq), F32),
                        pltpu.VMEM((ATTN_HEADS, MLA_V + SUBLANE, tq), F32)],
        compiler_params=_params("parallel", "parallel", "arbitrary"),
        name="mla_attention",
    )(q, k, v)


def _rwkv_prep_kernel(has_vres, *refs):
    if has_vres:
        (x_ref, gmix_ref, w_ref, mu_ref, w0_ref, w2_ref, a0_ref, a2_ref, g2_ref, kk_ref, ka_ref,
         vfirst_ref, v0_ref, v1_ref, v2_ref,
         r_out, lw_out, k_out, v_out, kk_out, a_out, g_out, carry_ref) = refs
    else:
        (x_ref, gmix_ref, w_ref, mu_ref, w0_ref, w2_ref, a0_ref, a2_ref, g2_ref, kk_ref, ka_ref,
         r_out, lw_out, k_out, v_out, kk_out, a_out, g_out, carry_ref) = refs

    @pl.when(pl.program_id(1) == 0)
    def _():
        carry_ref[...] = jnp.zeros_like(carry_ref)

    h = _rms(x_ref[...], gmix_ref[...], NORM_EPS).astype(BF16)
    tm = h.shape[0]
    d = RWKV_DIM

    def mixed(lo, hi):
        p = _dot(h, w_ref[:, lo:hi])
        prev = jnp.where(_iota2(p.shape, 0) == 0, carry_ref[:, lo:hi], pltpu.roll(p, 1, 0))
        carry_ref[:, lo:hi] = p[tm - 1:tm, :]
        return p + (prev - p) * mu_ref[:, lo:hi]

    lora = mixed(3 * d, RWKV_IN)
    xwa = lora[:, :LANE]
    r_out[...] = mixed(0, d)
    w = -_softplus(-(w0_ref[...] + _dot(jnp.tanh(xwa), w2_ref[...]))) - 0.5
    lw_out[...] = -jnp.exp(w)
    a = _sigmoid(a0_ref[...] + _dot(xwa, a2_ref[...]))
    a_out[...] = a
    g_out[...] = _dot(_sigmoid(lora[:, LANE:]), g2_ref[...])
    k = mixed(d, 2 * d)
    kk_out[...] = k * kk_ref[...]
    k_out[...] = k * (1.0 + (a - 1.0) * ka_ref[...])
    v = mixed(2 * d, 3 * d)
    if has_vres:
        gate = _sigmoid(v0_ref[...] + _dot(_dot(v, v1_ref[...]), v2_ref[...]))
        v = v + (vfirst_ref[...] - v) * gate
    v_out[...] = v


def _head_sum(x):
    blocks = []
    for j in range(x.shape[-1] // LANE):
        xb = x[..., LANE * j:LANE * (j + 1)]
        first = _iota2(xb.shape, xb.ndim - 1) < RWKV_HEAD
        s0 = jnp.sum(jnp.where(first, xb, 0.0), axis=-1, keepdims=True)
        s1 = jnp.sum(jnp.where(first, 0.0, xb), axis=-1, keepdims=True)
        blocks.append(jnp.where(first, s0, s1))
    return blocks[0] if len(blocks) == 1 else jnp.concatenate(blocks, axis=-1)


def _rwkv_scan_kernel(r_ref, lw_ref, k_ref, v_ref, kk_ref, a_ref, g_ref, rk_ref, lng_ref, lnb_ref,
                      o_ref, state_ref, y_ref):
    L = RWKV_CHUNK
    n = RWKV_HEAD
    ts, width = r_ref.shape
    nh = width // n
    nchunk = ts // L
    sh3 = (nchunk, L, width)

    @pl.when(pl.program_id(2) == 0)
    def _():
        state_ref[...] = jnp.zeros_like(state_ref)

    t_idx = _iota2(sh3, 1)
    r = r_ref[...].reshape(sh3)
    lw = lw_ref[...].reshape(sh3)
    k = k_ref[...].reshape(sh3)
    v = v_ref[...].reshape(sh3)
    kk = kk_ref[...].reshape(sh3)
    a = a_ref[...].reshape(sh3)

    kk = kk / jnp.maximum(jnp.sqrt(_head_sum(kk * kk)), 1e-12)
    cum = lw
    s = 1
    while s < L:
        cum = cum + jnp.where(t_idx >= s, pltpu.roll(cum, s, 1), 0.0)
        s *= 2
    tot = cum[:, L - 1:L, :]
    inv = jnp.exp(-cum)
    to_end = jnp.exp(tot - cum)

    gw = RWKV_PACK * n
    ngroup = width // gw

    def groups_to_batch(x):
        return jnp.concatenate([x[:, :, gw * j:gw * (j + 1)] for j in range(ngroup)], axis=0)

    def block_diag(x, head_lane):
        xb = x.astype(BF16)
        zero = jnp.zeros_like(xb)
        return jnp.concatenate([jnp.where(head_lane == hh, xb, zero) for hh in range(RWKV_PACK)], axis=-2)

    at = groups_to_batch(-kk * jnp.exp(cum - lw))
    rt = groups_to_batch(r * jnp.exp(cum))
    bt = groups_to_batch(kk * a * inv)
    kt = groups_to_batch(k * inv)
    bend = groups_to_batch(kk * a * to_end).astype(BF16)
    kend = groups_to_batch(k * to_end).astype(BF16)
    vb = groups_to_batch(v)
    tot2 = tot.reshape(nchunk, width)
    pad = jnp.zeros((LANE - nchunk, LANE), F32)
    gam_t = jnp.concatenate([jnp.exp(jnp.concatenate([tot2[:, LANE * j:LANE * (j + 1)], pad], axis=0).T)
                             for j in range(width // LANE)], axis=0)

    nb = ngroup * nchunk
    shift = n.bit_length() - 1
    head3 = _iota2((nb, L, gw), 2) >> shift
    bd = functools.partial(block_diag, head_lane=head3)
    row = _iota2((nb, 2 * L, gw), 1)
    col = _iota2((nb, 2 * L, gw), 2) & (n - 1)
    causal = row - col >= jnp.where(row < L, 1, L)
    eye = (_iota2((nb, L, gw), 1) == (_iota2((nb, L, gw), 2) & (n - 1))).astype(F32)
    same_head = (_iota2((nb, gw, gw), 1) >> shift) == (_iota2((nb, gw, gw), 2) >> shift)

    ar = jnp.concatenate([at, rt], axis=1)
    pb = jnp.where(causal, _bmm_nt(ar, bd(bt)), 0.0)
    pk = jnp.where(causal, _bmm_nt(ar, bd(kt)), 0.0)
    wv = _bmm(pk, bd(vb))
    mab = pb[:, :L]
    tinv = eye + mab
    pw = _bmm(mab, bd(mab))
    s = 4
    while s < L:
        both = _bmm(jnp.concatenate([tinv, pw], axis=1), bd(pw))
        tinv = tinv + both[:, :L]
        pw = both[:, L:]
        s *= 2
    tinv = tinv + _bmm(tinv, bd(pw))
    abar = _bmm(tinv, bd(at))
    u0 = _bmm(tinv, bd(wv[:, :L]))
    wmat = jnp.where(same_head, _bmm_tn(bend, abar), 0.0).astype(BF16)
    nmat = jnp.where(same_head, _bmm_tn(jnp.concatenate([bend, kend], axis=1),
                                        jnp.concatenate([u0, vb], axis=1)), 0.0)
    ar = jnp.concatenate([abar, rt], axis=1).astype(BF16)
    nrb = pb[:, L:].astype(BF16)
    y0 = wv[:, L:]

    gs = range(ngroup)
    head2 = _iota2((L, gw), 1) >> shift
    st = [state_ref[j] for j in gs]
    for c in range(nchunk):
        idx = [j * nchunk + c for j in gs]
        upd = [_dot(wmat[idx[j]], st[j]) for j in gs]
        rd = [_dot(ar[idx[j]], st[j]) for j in gs]
        for j in gs:
            u = rd[j][:L] + u0[idx[j]]
            y_ref[L * c:L * (c + 1), gw * j:gw * (j + 1)] = (
                rd[j][L:] + _dot(nrb[idx[j]], block_diag(u, head2)) + y0[idx[j]])
            st[j] = st[j] * gam_t[gw * j:gw * (j + 1), c:c + 1] + upd[j] + nmat[idx[j]]
    for j in gs:
        state_ref[j] = st[j]

    y = y_ref[...]
    mean = _head_sum(y) * (1.0 / n)
    var = _head_sum(jnp.square(y - mean)) * (1.0 / n)
    y = (y - mean) * lax.rsqrt(var + RWKV_GN_EPS)
    y = y * lng_ref[...] + lnb_ref[...]
    y = y + _head_sum(r_ref[...] * k_ref[...] * rk_ref[...]) * v_ref[...]
    o_ref[...] = (y * g_ref[...]).astype(BF16)


def _rwkv_branch(x2, bsz, seq, gmix, wp, v_first):
    n_tok = x2.shape[0]
    tm = min(TOK_TILE, seq)
    ns = seq // tm
    has_vres = v_first is not None
    d = RWKV_DIM
    tok = lambda w: pl.BlockSpec((tm, w), lambda b, i: (b * ns + i, 0))
    vec = _full((1, d))
    in_specs = [tok(D_MODEL), _full((1, D_MODEL)), _full((D_MODEL, RWKV_IN)), _full((1, RWKV_IN)), vec,
                _full((LANE, d)), vec, _full((LANE, d)), _full((RWKV_G_RANK, d)), vec, vec]
    args = [x2, gmix, wp["w_rwkv"], wp["mu"], wp["w0"], wp["w2"], wp["a0"], wp["a2"], wp["g2"], wp["k_k"], wp["k_a"]]
    if has_vres:
        in_specs += [tok(d), vec, _full((d, LANE)), _full((LANE, d))]
        args += [v_first, wp["v0"], wp["v1"], wp["v2"]]
    r, lw, k, v, kk, a, g = pl.pallas_call(
        functools.partial(_rwkv_prep_kernel, has_vres),
        grid=(bsz, ns),
        in_specs=in_specs,
        out_specs=[tok(d)] * 7,
        out_shape=[jax.ShapeDtypeStruct((n_tok, d), F32)] * 7,
        scratch_shapes=[pltpu.VMEM((1, RWKV_IN), F32)],
        compiler_params=_params("parallel", "arbitrary"),
        name="rwkv_prep",
    )(*args)

    ts = min(RWKV_TILE, seq)
    nt = seq // ts
    width = RWKV_SCAN_HEADS * RWKV_HEAD
    blk = pl.BlockSpec((ts, width), lambda b, hp, i: (b * nt + i, hp))
    par = pl.BlockSpec((1, width), lambda b, hp, i: (0, hp))
    o = pl.pallas_call(
        _rwkv_scan_kernel,
        grid=(bsz, RWKV_HEADS // RWKV_SCAN_HEADS, nt),
        in_specs=[blk] * 7 + [par] * 3,
        out_specs=blk,
        out_shape=jax.ShapeDtypeStruct((n_tok, d), BF16),
        scratch_shapes=[pltpu.VMEM((RWKV_SCAN_HEADS // RWKV_PACK, RWKV_PACK * RWKV_HEAD, RWKV_PACK * RWKV_HEAD), F32),
                        pltpu.VMEM((ts, width), F32)],
        compiler_params=_params("parallel", "parallel", "arbitrary"),
        name="rwkv_scan",
    )(r, lw, k, v, kk, a, g, wp["r_k"], wp["ln_g"], wp["ln_b"])
    return o, v


def _ssm_prep_kernel(x_ref, gmix_ref, w_ref, cw_ref, cb_ref, dtb_ref,
                     z_out, xs_out, b_out, c_out, dt_out, carry_ref):
    @pl.when(pl.program_id(1) == 0)
    def _():
        carry_ref[...] = jnp.zeros_like(carry_ref)

    h = _rms(x_ref[...], gmix_ref[...], NORM_EPS).astype(BF16)
    tm = h.shape[0]
    gw = SSM_CONV_GROUP
    top_row = _iota2((SUBLANE, gw), 0)
    for j in range(SSM_CONV_DIM // gw):
        cs = slice(gw * j, gw * (j + 1))
        xbc = _dot(h, w_ref[:, SSM_DIM + gw * j:SSM_DIM + gw * (j + 1)])
        if gw * j < SSM_DIM:
            z_out[:, cs] = _dot(h, w_ref[:, cs])
        else:
            dt_out[...] = _softplus(_dot(h, w_ref[:, SSM_DIM + SSM_CONV_DIM:]) + dtb_ref[...])
        prev = carry_ref[:, cs]
        carry_ref[:, cs] = xbc[tm - SUBLANE:, :]
        acc = xbc * cw_ref[SSM_CONV - 1:SSM_CONV, cs] + cb_ref[:, cs]
        for s in range(1, SSM_CONV):
            rolled = pltpu.roll(xbc, s, 0)
            top = jnp.where(top_row < s, pltpu.roll(prev, s, 0), rolled[:SUBLANE])
            shifted = jnp.concatenate([top, rolled[SUBLANE:]], axis=0)
            acc = acc + shifted * cw_ref[SSM_CONV - 1 - s:SSM_CONV - s, cs]
        act = acc * _sigmoid(acc)
        if gw * j < SSM_DIM:
            xs_out[:, cs] = act
        else:
            gn = SSM_GROUPS * SSM_STATE
            b_out[...] = act[:, :gn].astype(BF16)
            c_out[...] = act[:, gn:].astype(BF16)


def _ssd_kernel(xs_ref, b_ref, c_ref, dt_ref, z_ref, alog_ref, d_ref, ng_ref, o_ref, state_ref, y_ref):
    @pl.when(pl.program_id(1) == 0)
    def _():
        state_ref[...] = jnp.zeros_like(state_ref)

    L = xs_ref.shape[0]
    row = _iota2((L, L), 0)
    col = _iota2((L, L), 1)
    lower = row >= col
    tril = lower.astype(F32)
    eye = (_iota2((LANE, LANE), 0) == _iota2((LANE, LANE), 1)).astype(F32)

    dt = dt_ref[...]
    da = dt * (-jnp.exp(alog_ref[...]))
    acs = _dot_exact(tril, da)
    acs_t = _dot_nt_exact(eye, acs)
    dt_t = _dot_nt_exact(eye, dt)
    first = _iota2((1, LANE), 1) < SSM_HEAD

    for g in range(SSM_GROUPS):
        gs = slice(SSM_STATE * g, SSM_STATE * (g + 1))
        bm = b_ref[:, gs]
        cm = c_ref[:, gs]
        cb16 = _dot_nt(cm, bm).astype(BF16)
        bm_t = _dot_nt(eye, bm)
        for e in range(0, SSM_HPG, 2):
            pair = (SSM_HPG * g + e) // 2
            ps = slice(LANE * pair, LANE * (pair + 1))
            xs = xs_ref[:, ps]
            xs16 = xs.astype(BF16)
            st = state_ref[pair]
            y_in = _dot(cm, st)
            ys, sts = [], []
            for hd in (2 * pair, 2 * pair + 1):
                a_col = jnp.broadcast_to(acs[:, hd:hd + 1], (L, LANE))
                a_row = acs_t[hd:hd + 1, :]
                dt_row = dt_t[hd:hd + 1, :]
                tot = acs[L - 1:L, hd:hd + 1]
                seg = jnp.concatenate([jnp.exp(a_col - a_row[:, LANE * j:LANE * (j + 1)])
                                       for j in range(L // LANE)], axis=1)
                w = cb16 * jnp.where(lower, seg, 0.0).astype(BF16) * dt_row.astype(BF16)
                ys.append(_dot(w, xs16) + y_in * jnp.exp(a_col))
                sts.append(st * jnp.exp(tot) + _dot(bm_t * (dt_row * jnp.exp(tot - a_row)), xs16))
            y_ref[:, ps] = jnp.where(first, ys[0], ys[1]) + xs * d_ref[:, ps]
            state_ref[pair] = jnp.where(first, sts[0], sts[1])

    z = z_ref[...]
    y = y_ref[...] * (z * _sigmoid(z))
    gw = SSM_DIM // SSM_GROUPS
    for g in range(SSM_GROUPS):
        gs = slice(gw * g, gw * (g + 1))
        yg = y[:, gs]
        yg = yg * lax.rsqrt(jnp.mean(yg * yg, axis=-1, keepdims=True) + SSM_NORM_EPS)
        o_ref[:, gs] = (yg * ng_ref[:, gs]).astype(BF16)


def _ssm_branch(x2, bsz, seq, gmix, wp):
    n_tok = x2.shape[0]
    tm = min(TOK_TILE, seq)
    ns = seq // tm
    gn = SSM_GROUPS * SSM_STATE
    tok = lambda w: pl.BlockSpec((tm, w), lambda b, i: (b * ns + i, 0))
    z, xs, bm, cm, dt = pl.pallas_call(
        _ssm_prep_kernel,
        grid=(bsz, ns),
        in_specs=[tok(D_MODEL), _full((1, D_MODEL)), _full(wp["w_ssm"].shape), _full((SSM_CONV, SSM_CONV_DIM)),
                  _full((1, SSM_CONV_DIM)), _full((1, LANE))],
        out_specs=[tok(SSM_DIM), tok(SSM_DIM), tok(gn), tok(gn), tok(LANE)],
        out_shape=[jax.ShapeDtypeStruct((n_tok, SSM_DIM), F32), jax.ShapeDtypeStruct((n_tok, SSM_DIM), F32),
                   jax.ShapeDtypeStruct((n_tok, gn), BF16), jax.ShapeDtypeStruct((n_tok, gn), BF16),
                   jax.ShapeDtypeStruct((n_tok, LANE), F32)],
        scratch_shapes=[pltpu.VMEM((SUBLANE, SSM_CONV_DIM), F32)],
        compiler_params=_params("parallel", "arbitrary"),
        name="ssm_prep",
    )(x2, gmix, wp["w_ssm"], wp["conv_w"], wp["conv_b"], wp["dt_bias"])

    assert seq % SSM_CHUNK == 0
    nc = seq // SSM_CHUNK
    ch = lambda w: pl.BlockSpec((SSM_CHUNK, w), lambda b, c: (b * nc + c, 0))
    return pl.pallas_call(
        _ssd_kernel,
        grid=(bsz, nc),
        in_specs=[ch(SSM_DIM), ch(gn), ch(gn), ch(LANE), ch(SSM_DIM), _full((1, LANE)), _full((1, SSM_DIM)),
                  _full((1, SSM_DIM))],
        out_specs=ch(SSM_DIM),
        out_shape=jax.ShapeDtypeStruct((n_tok, SSM_DIM), BF16),
        scratch_shapes=[pltpu.VMEM((SSM_HEADS // 2, SSM_STATE, 2 * SSM_HEAD), F32),
                        pltpu.VMEM((SSM_CHUNK, SSM_DIM), F32)],
        compiler_params=_params("parallel", "arbitrary"),
        name="ssd",
    )(xs, bm, cm, dt, z, wp["a_log"], wp["d_skip"], wp["norm_g"])


def _merge_kernel(x_ref, gmix_ref, wg_ref, om_ref, or_ref, os_ref, wm_ref, wr_ref, ws_ref, wo_ref, o_ref):
    x = x_ref[...]
    h = _rms(x, gmix_ref[...], NORM_EPS).astype(BF16)
    d = D_MODEL
    merged = _sigmoid(_dot(h, wg_ref[:, :d])) * _dot(om_ref[...], wm_ref[...])
    merged = merged + _sigmoid(_dot(h, wg_ref[:, d:2 * d])) * _dot(or_ref[...], wr_ref[...])
    merged = merged + _sigmoid(_dot(h, wg_ref[:, 2 * d:])) * _dot(os_ref[...], ws_ref[...])
    o_ref[...] = x + _dot(merged, wo_ref[...])


def _ffn_kernel(x_ref, g_ref, w1_ref, w2_ref, o_ref, h_ref):
    j = pl.program_id(1)

    @pl.when(j == 0)
    def _():
        x = x_ref[...]
        h_ref[...] = _rms(x, g_ref[...], NORM_EPS).astype(BF16)
        o_ref[...] = x

    u = jnp.maximum(_dot(h_ref[...], w1_ref[...]), 0.0)
    o_ref[...] += _dot(u * u, w2_ref[...])


def _merge_ffn(x2, gmix, wp, o_mla, o_rwkv, o_ssm):
    n_tok = x2.shape[0]
    tm = min(TOK_TILE, n_tok)
    tok = lambda w: pl.BlockSpec((tm, w), lambda i: (i, 0))
    x2 = pl.pallas_call(
        _merge_kernel,
        grid=(n_tok // tm,),
        in_specs=[tok(D_MODEL), _full((1, D_MODEL)), _full((D_MODEL, GATE_IN)), tok(o_mla.shape[1]),
                  tok(o_rwkv.shape[1]), tok(o_ssm.shape[1]), _full(wp["w_br_mla"].shape),
                  _full(wp["w_br_rwkv"].shape), _full(wp["w_br_ssm"].shape), _full((D_MODEL, D_MODEL))],
        out_specs=tok(D_MODEL),
        out_shape=jax.ShapeDtypeStruct((n_tok, D_MODEL), F32),
        compiler_params=_params("parallel"),
        name="merge",
    )(x2, gmix, wp["w_gate"], o_mla, o_rwkv, o_ssm, wp["w_br_mla"], wp["w_br_rwkv"], wp["w_br_ssm"], wp["w_out"])

    tf = min(FFN_TOK_TILE, n_tok)
    ff = FFN_FF_TILE
    return pl.pallas_call(
        _ffn_kernel,
        grid=(n_tok // tf, D_FF // ff),
        in_specs=[pl.BlockSpec((tf, D_MODEL), lambda i, j: (i, 0)), _full((1, D_MODEL)),
                  pl.BlockSpec((D_MODEL, ff), lambda i, j: (0, j)), pl.BlockSpec((ff, D_MODEL), lambda i, j: (j, 0))],
        out_specs=pl.BlockSpec((tf, D_MODEL), lambda i, j: (i, 0)),
        out_shape=jax.ShapeDtypeStruct((n_tok, D_MODEL), F32),
        scratch_shapes=[pltpu.VMEM((tf, D_MODEL), BF16)],
        compiler_params=_params("parallel", "arbitrary"),
        name="ffn",
    )(x2, wp["norm_ffn_g"], wp["w_ff1"], wp["w_ff2"])


def _pad_cols(w, width):
    return jnp.pad(w, ((0, 0), (0, width - w.shape[1])))


def _pad_rows(w, height, before=0):
    return jnp.pad(w, ((before, height - before - w.shape[0]), (0, 0)))


def _row(v, width=None):
    v = v.reshape(1, -1).astype(F32)
    return v if width is None else _pad_cols(v, width)


def _layer_weights(l, p):
    w_in = p["w_in"][l]
    o_r = MLA_IN
    o_s = o_r + RWKV_IN
    o_g = o_s + SSM_IN
    zc = lambda n: jnp.zeros((D_MODEL, n), w_in.dtype)
    w_mla = jnp.concatenate([w_in[:, :MLA_Q_RANK + MLA_KV_RANK], zc(MLA_NOPE), w_in[:, MLA_Q_RANK + MLA_KV_RANK:MLA_IN],
                             zc(LANE - MLA_QK)], axis=1)
    half = MLA_ROPE // 2

    def slot(t, rotate):
        lo, hi = t[..., MLA_NOPE:MLA_NOPE + half], t[..., MLA_NOPE + half:]
        parts = [jnp.zeros_like(t[..., :MLA_NOPE]), -hi if rotate == "signed" else hi, lo] if rotate else [t]
        t = jnp.concatenate(parts, axis=-1)
        return jnp.pad(t, [(0, 0)] * (t.ndim - 1) + [(0, LANE - MLA_QK)])

    w_uq = p["mla_w_uq"][l].reshape(MLA_Q_RANK, MLA_HEADS, MLA_QK)
    w_uq = jnp.concatenate([slot(w_uq, None).reshape(MLA_Q_RANK, MLA_HEADS * LANE),
                            slot(w_uq, "signed").reshape(MLA_Q_RANK, MLA_HEADS * LANE)], axis=1)
    q_head_g = p["mla_q_head_g"][l].astype(F32)
    q_head_g = jnp.stack([slot(q_head_g, None), slot(q_head_g, "moved")])
    w_ukv = p["mla_w_ukv"][l].reshape(MLA_KV_RANK, MLA_HEADS, MLA_NOPE + MLA_V)
    w_uk = jnp.pad(w_ukv[:, :, :MLA_NOPE], ((0, 0), (0, 0), (0, LANE - MLA_NOPE))).reshape(MLA_KV_RANK, MLA_HEADS * LANE)
    w_uv = jnp.pad(w_ukv[:, :, MLA_NOPE:], ((0, 0), (0, 0), (0, LANE - MLA_V))).reshape(MLA_KV_RANK, MLA_HEADS * LANE)
    w_ssm = _pad_cols(w_in[:, o_s:o_g], SSM_DIM + SSM_CONV_DIM + LANE)
    wp = {
        "w_mla": w_mla.astype(BF16), "w_uq": w_uq.astype(BF16), "w_uk": w_uk.astype(BF16), "w_uv": w_uv.astype(BF16),
        "q_norm_g": _row(p["mla_q_norm_g"][l]), "kv_norm_g": _row(p["mla_kv_norm_g"][l]),
        "q_head_g": q_head_g, "k_head_g": _row(p["mla_k_head_g"][l], LANE),
        "w_rwkv": w_in[:, o_r:o_s].astype(BF16), "mu": _row(p["rwkv_mu"][l]), "w0": _row(p["rwkv_w0"][l]),
        "w2": _pad_rows(p["rwkv_w2"][l], LANE).astype(BF16), "a0": _row(p["rwkv_a0"][l]),
        "a2": _pad_rows(p["rwkv_a2"][l], LANE, before=RWKV_W_RANK).astype(BF16), "g2": p["rwkv_g2"][l].astype(BF16),
        "k_k": _row(p["rwkv_k_k"][l]), "k_a": _row(p["rwkv_k_a"][l]), "r_k": _row(p["rwkv_r_k"][l]),
        "ln_g": _row(p["rwkv_ln_g"][l]), "ln_b": _row(p["rwkv_ln_b"][l]),
        "w_ssm": w_ssm.astype(BF16), "conv_w": p["ssm_conv_w"][l].astype(F32), "conv_b": _row(p["ssm_conv_b"][l]),
        "dt_bias": _row(p["ssm_dt_bias"][l], LANE), "a_log": _row(p["ssm_a_log"][l], LANE),
        "d_skip": _row(jnp.repeat(p["ssm_d"][l], SSM_HEAD)), "norm_g": _row(p["ssm_norm_g"][l]),
        "w_gate": w_in[:, o_g:].astype(BF16), "w_br_mla": p["w_br_mla"][l].astype(BF16),
        "w_br_rwkv": p["w_br_rwkv"][l].astype(BF16), "w_br_ssm": p["w_br_ssm"][l].astype(BF16),
        "w_out": p["w_out"][l].astype(BF16), "norm_ffn_g": _row(p["norm_ffn_g"][l]),
        "w_ff1": p["w_ff1"][l].astype(BF16), "w_ff2": p["w_ff2"][l].astype(BF16),
    }
    if l > 0:
        wp["v0"] = _row(p["rwkv_v0"][l - 1])
        wp["v1"] = _pad_cols(p["rwkv_v1"][l - 1], LANE).astype(BF16)
        wp["v2"] = _pad_rows(p["rwkv_v2"][l - 1], LANE).astype(BF16)
    return wp


def kernel(x, positions, norm_mix_g, w_in, mla_q_norm_g, mla_kv_norm_g, mla_w_uq, mla_w_ukv, mla_q_head_g, mla_k_head_g, rwkv_mu, rwkv_w0, rwkv_w2, rwkv_a0, rwkv_a2, rwkv_g2, rwkv_v0, rwkv_v1, rwkv_v2, rwkv_k_k, rwkv_k_a, rwkv_r_k, rwkv_ln_g, rwkv_ln_b, ssm_conv_w, ssm_conv_b, ssm_dt_bias, ssm_a_log, ssm_d, ssm_norm_g, w_br_mla, w_br_rwkv, w_br_ssm, w_out, norm_ffn_g, w_ff1, w_ff2):
    p = dict(w_in=w_in, mla_q_norm_g=mla_q_norm_g, mla_kv_norm_g=mla_kv_norm_g, mla_w_uq=mla_w_uq,
             mla_w_ukv=mla_w_ukv, mla_q_head_g=mla_q_head_g, mla_k_head_g=mla_k_head_g, rwkv_mu=rwkv_mu,
             rwkv_w0=rwkv_w0, rwkv_w2=rwkv_w2, rwkv_a0=rwkv_a0, rwkv_a2=rwkv_a2, rwkv_g2=rwkv_g2, rwkv_v0=rwkv_v0,
             rwkv_v1=rwkv_v1, rwkv_v2=rwkv_v2, rwkv_k_k=rwkv_k_k, rwkv_k_a=rwkv_k_a, rwkv_r_k=rwkv_r_k,
             rwkv_ln_g=rwkv_ln_g, rwkv_ln_b=rwkv_ln_b, ssm_conv_w=ssm_conv_w, ssm_conv_b=ssm_conv_b,
             ssm_dt_bias=ssm_dt_bias, ssm_a_log=ssm_a_log, ssm_d=ssm_d, ssm_norm_g=ssm_norm_g, w_br_mla=w_br_mla,
             w_br_rwkv=w_br_rwkv, w_br_ssm=w_br_ssm, w_out=w_out, norm_ffn_g=norm_ffn_g, w_ff1=w_ff1, w_ff2=w_ff2)
    bsz, seq, d_model = x.shape
    depth = w_in.shape[0]
    x2 = x.reshape(bsz * seq, d_model)
    tables = _rope_tables(positions)
    v_first = None
    for l in range(depth):
        wp = _layer_weights(l, p)
        gmix = _row(norm_mix_g[l])
        o_mla = _mla_branch(x2, bsz, seq, gmix, wp, tables)
        o_rwkv, v = _rwkv_branch(x2, bsz, seq, gmix, wp, v_first)
        if l == 0:
            v_first = v
        o_ssm = _ssm_branch(x2, bsz, seq, gmix, wp)
        x2 = _merge_ffn(x2, gmix, wp, o_mla, o_rwkv, o_ssm)
    return x2.reshape(bsz, seq, d_model)
```

```python
import functools

import jax
import jax.numpy as jnp
from jax import lax
from jax.experimental import pallas as pl
from jax.experimental.pallas import tpu as pltpu

F32 = jnp.float32
BF16 = jnp.bfloat16

D_MODEL = 1024
MLA_HEADS = 8
MLA_NOPE = 64
MLA_ROPE = 32
MLA_QK = MLA_NOPE + MLA_ROPE
MLA_V = 64
MLA_Q_RANK = 384
MLA_KV_RANK = 256
ROPE_THETA = 10000.0
RWKV_HEAD = 64
RWKV_HEADS = 8
RWKV_DIM = RWKV_HEADS * RWKV_HEAD
RWKV_W_RANK = 64
RWKV_A_RANK = 64
RWKV_V_RANK = 32
RWKV_G_RANK = 128
RWKV_GN_EPS = 64e-5
SSM_HEAD = 64
SSM_HEADS = 16
SSM_DIM = SSM_HEADS * SSM_HEAD
SSM_GROUPS = 2
SSM_HPG = SSM_HEADS // SSM_GROUPS
SSM_STATE = 128
SSM_CONV = 4
SSM_CHUNK = 256
SSM_CONV_DIM = SSM_DIM + 2 * SSM_GROUPS * SSM_STATE
SSM_NORM_EPS = 1e-5
N_BRANCH = 3
D_FF = 4 * D_MODEL
NORM_EPS = 1e-6
MLA_IN = MLA_Q_RANK + MLA_KV_RANK + MLA_ROPE
RWKV_IN = 3 * RWKV_DIM + RWKV_W_RANK + RWKV_A_RANK + RWKV_G_RANK
SSM_IN = SSM_DIM + SSM_CONV_DIM + SSM_HEADS
GATE_IN = N_BRANCH * D_MODEL

LOG2_E = 1.4426950408889634
LANE = 128
SUBLANE = 8
VMEM_LIMIT = 56 * 1024 * 1024

TOK_TILE = 512
MLA_PREP_PARTS = 2
FFN_TOK_TILE = 1024
FFN_FF_TILE = 1024
ATTN_TILE = 512
ATTN_HEADS = 8
ATTN_KEYS = 256
SSM_CONV_GROUP = 512
RWKV_CHUNK = 64
RWKV_TILE = 512
RWKV_SCAN_HEADS = 8
RWKV_PACK = 4


def _dot(a, b):
    return jnp.dot(a.astype(BF16), b.astype(BF16), preferred_element_type=F32)


def _dot_nt(a, b):
    return lax.dot_general(a.astype(BF16), b.astype(BF16), (((1,), (1,)), ((), ())),
                           preferred_element_type=F32)


def _dot_tn(a, b):
    return lax.dot_general(a.astype(BF16), b.astype(BF16), (((0,), (0,)), ((), ())),
                           preferred_element_type=F32)


def _bmm(a, b):
    return lax.dot_general(a.astype(BF16), b.astype(BF16), (((2,), (1,)), ((0,), (0,))),
                           preferred_element_type=F32)


def _bmm_nt(a, b):
    return lax.dot_general(a.astype(BF16), b.astype(BF16), (((2,), (2,)), ((0,), (0,))),
                           preferred_element_type=F32)


def _bmm_tn(a, b):
    return lax.dot_general(a.astype(BF16), b.astype(BF16), (((1,), (1,)), ((0,), (0,))),
                           preferred_element_type=F32)


def _dot_exact(a, b):
    return jnp.dot(a, b, preferred_element_type=F32, precision=lax.Precision.HIGHEST)


def _dot_nt_exact(a, b):
    return lax.dot_general(a, b, (((1,), (1,)), ((), ())), preferred_element_type=F32,
                           precision=lax.Precision.HIGHEST)


def _rms(x, g, eps):
    return x * lax.rsqrt(jnp.mean(x * x, axis=-1, keepdims=True) + eps) * g


def _sigmoid(x):
    return 1.0 / (1.0 + jnp.exp(-x))


def _softplus(x):
    return jnp.maximum(x, 0.0) + jnp.log(1.0 + jnp.exp(-jnp.abs(x)))


def _iota2(shape, dim):
    return lax.broadcasted_iota(jnp.int32, shape, dim)


def _params(*sem):
    return pltpu.CompilerParams(dimension_semantics=sem, vmem_limit_bytes=VMEM_LIMIT)


def _full(shape):
    return pl.BlockSpec(shape, lambda *_: (0,) * len(shape))


def _rope_table_kernel(pos_ref, freq_ref, cos_ref, sin_ref):
    ang = pos_ref[...].astype(F32) * freq_ref[...]
    cos_ref[...] = jnp.cos(ang)
    sin_ref[...] = jnp.sin(ang)


def _rope_tables(positions):
    half = MLA_ROPE // 2
    n_tok = positions.size
    inv_freq = ROPE_THETA ** (-jnp.arange(half, dtype=jnp.float32) / half)
    per_row = LANE // half
    rows = n_tok // per_row
    pos = jnp.broadcast_to(positions.reshape(rows, per_row, 1), (rows, per_row, half)).reshape(rows, LANE)
    freq = jnp.tile(inv_freq, per_row).reshape(1, LANE)
    blk = min(rows, 512)
    cos, sin = pl.pallas_call(
        _rope_table_kernel,
        grid=(rows // blk,),
        in_specs=[pl.BlockSpec((blk, LANE), lambda i: (i, 0)), _full((1, LANE))],
        out_specs=[pl.BlockSpec((blk, LANE), lambda i: (i, 0))] * 2,
        out_shape=[jax.ShapeDtypeStruct((rows, LANE), F32)] * 2,
        compiler_params=_params("parallel"),
        name="rope_table",
    )(pos, freq)
    cos = cos.reshape(n_tok, half)
    sin = sin.reshape(n_tok, half)
    one = jnp.ones((n_tok, MLA_NOPE), F32)
    z_nope = jnp.zeros((n_tok, MLA_NOPE), F32)
    z_half = jnp.zeros((n_tok, half), F32)
    z_pad = jnp.zeros((n_tok, LANE - MLA_QK), F32)
    cosf = jnp.concatenate([one, cos, cos, z_pad], axis=1)
    sina = jnp.concatenate([z_nope, -sin, z_half, z_pad], axis=1)
    sinb = jnp.concatenate([z_nope, z_half, sin, z_pad], axis=1)
    return cosf, sina, sinb


def _mla_prep_kernel(x_ref, gmix_ref, wmla_ref, qg_ref, kvg_ref, wuq_ref, wuk_ref, wuv_ref,
                     qhg_ref, khg_ref, cos_ref, sina_ref, sinb_ref, q_ref, k_ref, v_ref):
    hq = MLA_HEADS * LANE
    tm = x_ref.shape[0]
    parts = [slice(tm * i // MLA_PREP_PARTS, tm * (i + 1) // MLA_PREP_PARTS) for i in range(MLA_PREP_PARTS)]
    one_lane = (_iota2((1, hq), 1) & (LANE - 1)) == MLA_V
    khg = khg_ref[...]
    scale = MLA_QK ** -0.5 * LOG2_E

    def inv_rms(t):
        return lax.rsqrt(jnp.sum(t * t, axis=-1, keepdims=True) * (1.0 / MLA_QK) + NORM_EPS)

    latent = [_dot(_rms(x_ref[rs, :], gmix_ref[...], NORM_EPS), wmla_ref[...])
              for rs in parts]
    up = []
    for rs, c in zip(parts, latent):
        cq = _rms(c[:, :MLA_Q_RANK], qg_ref[...], NORM_EPS)
        ckv = _rms(c[:, MLA_Q_RANK:MLA_Q_RANK + MLA_KV_RANK], kvg_ref[...], NORM_EPS)
        qall = _dot(cq, wuq_ref[...])
        kf = _dot(ckv, wuk_ref[...])
        v_ref[rs, :] = jnp.where(one_lane, 1.0, _dot(ckv, wuv_ref[...])).astype(BF16)
        up.append((qall, kf))
    for rs, c, (qall, kf) in zip(parts, latent, up):
        krope = c[:, MLA_Q_RANK + MLA_KV_RANK:]
        cosf, sina, sinb = cos_ref[rs, :], sina_ref[rs, :], sinb_ref[rs, :]
        q_cos = qhg_ref[0:1, :] * cosf
        q_sin = qhg_ref[1:2, :] * (sinb - sina)
        k_cos = khg * cosf
        kg = krope * khg
        k_rot = pltpu.roll(kg, LANE - MLA_ROPE // 2, 1) * sina + pltpu.roll(kg, MLA_ROPE // 2, 1) * sinb
        for hd in range(MLA_HEADS):
            sl = slice(LANE * hd, LANE * (hd + 1))
            qh = qall[:, sl]
            qr = qall[:, hq + LANE * hd:hq + LANE * (hd + 1)]
            q_ref[rs, sl] = ((qh * q_cos + qr * q_sin) * (inv_rms(qh) * scale)).astype(BF16)
            kh = kf[:, sl] + krope
            k_ref[rs, sl] = ((kh * k_cos + k_rot) * inv_rms(kh)).astype(BF16)


def _attn_kernel(q_ref, k_ref, v_ref, o_ref, m_ref, acc_ref):
    i = pl.program_id(2)
    tq = q_ref.shape[0]
    heads = range(ATTN_HEADS)
    acc_rows = acc_ref.shape[1]
    m_ref[...] = jnp.full(m_ref.shape, -jnp.inf, F32)
    acc_ref[...] = jnp.zeros(acc_ref.shape, F32)

    def step(j, masked):
        for sub in range(tq // ATTN_KEYS):
            substep(j * tq + sub * ATTN_KEYS, sub * ATTN_KEYS if masked else 0, masked)

    def substep(start, q0, masked):
        rows = pl.ds(pl.multiple_of(start, ATTN_KEYS), ATTN_KEYS)
        qs = slice(q0, tq)
        scores = [_dot_nt(k_ref[rows, LANE * hh:LANE * (hh + 1)], q_ref[qs, LANE * hh:LANE * (hh + 1)])
                  for hh in heads]
        probs = []
        for hh in heads:
            s = scores[hh]
            if masked:
                s = jnp.where(_iota2(s.shape, 1) >= _iota2(s.shape, 0), s, -jnp.inf)
            m = m_ref[hh, :, qs]
            m_new = jnp.maximum(m, jnp.max(s, axis=0, keepdims=True))
            p = jnp.exp2(s - m_new)
            alpha = jnp.exp2(m - m_new)
            m_ref[hh, :, qs] = m_new
            probs.append((p.astype(BF16), alpha))
        for hh in heads:
            p, alpha = probs[hh]
            v = v_ref[rows, LANE * hh:LANE * (hh + 1)]
            acc_ref[hh, :, qs] = alpha * acc_ref[hh, :, qs] + _dot_tn(v, p)[:acc_rows]

    def body(j, carry):
        step(j, False)
        return carry

    lax.fori_loop(0, i, body, 0)
    step(i, True)
    out = jnp.concatenate([acc_ref[hh, :MLA_V, :] / acc_ref[hh, MLA_V:MLA_V + 1, :] for hh in heads],
                          axis=0)
    o_ref[...] = out.T.astype(BF16)


def _mla_branch(x2, bsz, seq, gmix, wp, tables):
    n_tok = x2.shape[0]
    tm = min(TOK_TILE, seq)
    cosf, sina, sinb = tables
    tok = lambda w: pl.BlockSpec((tm, w), lambda i: (i, 0))
    hq = MLA_HEADS * LANE
    q, k, v = pl.pallas_call(
        _mla_prep_kernel,
        grid=(n_tok // tm,),
        in_specs=[tok(D_MODEL), _full((1, D_MODEL)), _full(wp["w_mla"].shape), _full((1, MLA_Q_RANK)),
                  _full((1, MLA_KV_RANK)), _full(wp["w_uq"].shape), _full(wp["w_uk"].shape),
                  _full(wp["w_uv"].shape), _full((2, LANE)), _full((1, LANE)), tok(LANE), tok(LANE), tok(LANE)],
        out_specs=[tok(hq)] * 3,
        out_shape=[jax.ShapeDtypeStruct((n_tok, hq), BF16)] * 3,
        compiler_params=_params("parallel"),
        name="mla_prep",
    )(x2, gmix, wp["w_mla"], wp["q_norm_g"], wp["kv_norm_g"], wp["w_uq"], wp["w_uk"], wp["w_uv"],
      wp["q_head_g"], wp["k_head_g"], cosf, sina, sinb)

    tq = min(ATTN_TILE, seq)
    nq = seq // tq
    return pl.pallas_call(
        _attn_kernel,
        grid=(bsz, MLA_HEADS // ATTN_HEADS, nq),
        in_specs=[pl.BlockSpec((tq, ATTN_HEADS * LANE), lambda b, hp, i: (b * nq + i, hp)),
                  pl.BlockSpec((seq, ATTN_HEADS * LANE), lambda b, hp, i: (b, hp)),
                  pl.BlockSpec((seq, ATTN_HEADS * LANE), lambda b, hp, i: (b, hp))],
        out_specs=pl.BlockSpec((tq, ATTN_HEADS * MLA_V), lambda b, hp, i: (b * nq + i, hp)),
        out_shape=jax.ShapeDtypeStruct((n_tok, MLA_HEADS * MLA_V), BF16),
        scratch_shapes=[pltpu.VMEM((ATTN_HEADS, 1, tq), F32),
                        pltpu.VMEM((ATTN_HEADS, MLA_V + SUBLANE, tq), F32)],
        compiler_params=_params("parallel", "parallel", "arbitrary"),
        name="mla_attention",
    )(q, k, v)


def _rwkv_prep_kernel(has_vres, *refs):
    if has_vres:
        (x_ref, gmix_ref, w_ref, mu_ref, w0_ref, w2_ref, a0_ref, a2_ref, g2_ref, kk_ref, ka_ref,
         vfirst_ref, v0_ref, v1_ref, v2_ref,
         r_out, lw_out, k_out, v_out, kk_out, a_out, g_out, carry_ref) = refs
    else:
        (x_ref, gmix_ref, w_ref, mu_ref, w0_ref, w2_ref, a0_ref, a2_ref, g2_ref, kk_ref, ka_ref,
         r_out, lw_out, k_out, v_out, kk_out, a_out, g_out, carry_ref) = refs

    @pl.when(pl.program_id(1) == 0)
    def _():
        carry_ref[...] = jnp.zeros_like(carry_ref)

    h = _rms(x_ref[...], gmix_ref[...], NORM_EPS).astype(BF16)
    tm = h.shape[0]
    d = RWKV_DIM

    def mixed(lo, hi):
        p = _dot(h, w_ref[:, lo:hi])
        prev = jnp.where(_iota2(p.shape, 0) == 0, carry_ref[:, lo:hi], pltpu.roll(p, 1, 0))
        carry_ref[:, lo:hi] = p[tm - 1:tm, :]
        return p + (prev - p) * mu_ref[:, lo:hi]

    lora = mixed(3 * d, RWKV_IN)
    xwa = lora[:, :LANE]
    r_out[...] = mixed(0, d)
    w = -_softplus(-(w0_ref[...] + _dot(jnp.tanh(xwa), w2_ref[...]))) - 0.5
    lw_out[...] = -jnp.exp(w)
    a = _sigmoid(a0_ref[...] + _dot(xwa, a2_ref[...]))
    a_out[...] = a
    g_out[...] = _dot(_sigmoid(lora[:, LANE:]), g2_ref[...])
    k = mixed(d, 2 * d)
    kk_out[...] = k * kk_ref[...]
    k_out[...] = k * (1.0 + (a - 1.0) * ka_ref[...])
    v = mixed(2 * d, 3 * d)
    if has_vres:
        gate = _sigmoid(v0_ref[...] + _dot(_dot(v, v1_ref[...]), v2_ref[...]))
        v = v + (vfirst_ref[...] - v) * gate
    v_out[...] = v


def _head_sum(x):
    blocks = []
    for j in range(x.shape[-1] // LANE):
        xb = x[..., LANE * j:LANE * (j + 1)]
        first = _iota2(xb.shape, xb.ndim - 1) < RWKV_HEAD
        s0 = jnp.sum(jnp.where(first, xb, 0.0), axis=-1, keepdims=True)
        s1 = jnp.sum(jnp.where(first, 0.0, xb), axis=-1, keepdims=True)
        blocks.append(jnp.where(first, s0, s1))
    return blocks[0] if len(blocks) == 1 else jnp.concatenate(blocks, axis=-1)


def _rwkv_scan_kernel(r_ref, lw_ref, k_ref, v_ref, kk_ref, a_ref, g_ref, rk_ref, lng_ref, lnb_ref,
                      o_ref, state_ref, y_ref):
    L = RWKV_CHUNK
    n = RWKV_HEAD
    ts, width = r_ref.shape
    nchunk = ts // L
    sh3 = (nchunk, L, width)

    @pl.when(pl.program_id(2) == 0)
    def _():
        state_ref[...] = jnp.zeros_like(state_ref)

    t_idx = _iota2(sh3, 1)
    r = r_ref[...].reshape(sh3)
    lw = lw_ref[...].reshape(sh3)
    k = k_ref[...].reshape(sh3)
    v = v_ref[...].reshape(sh3)
    kk = kk_ref[...].reshape(sh3)
    a = a_ref[...].reshape(sh3)

    kk = kk / jnp.maximum(jnp.sqrt(_head_sum(kk * kk)), 1e-12)
    cum = lw
    s = 1
    while s < L:
        cum = cum + jnp.where(t_idx >= s, pltpu.roll(cum, s, 1), 0.0)
        s *= 2
    tot = cum[:, L - 1:L, :]
    inv = jnp.exp(-cum)
    to_end = jnp.exp(tot - cum)

    gw = RWKV_PACK * n
    ngroup = width // gw

    def groups_to_batch(x):
        return jnp.concatenate([x[:, :, gw * j:gw * (j + 1)] for j in range(ngroup)], axis=0)

    def block_diag(x, head_lane):
        xb = x.astype(BF16)
        zero = jnp.zeros_like(xb)
        return jnp.concatenate([jnp.where(head_lane == hh, xb, zero) for hh in range(RWKV_PACK)], axis=-2)

    at = groups_to_batch(-kk * jnp.exp(cum - lw))
    rt = groups_to_batch(r * jnp.exp(cum))
    bt = groups_to_batch(kk * a * inv)
    kt = groups_to_batch(k * inv)
    bend = groups_to_batch(kk * a * to_end).astype(BF16)
    kend = groups_to_batch(k * to_end).astype(BF16)
    vb = groups_to_batch(v)
    tot2 = tot.reshape(nchunk, width)
    pad = jnp.zeros((LANE - nchunk, LANE), F32)
    gam_t = jnp.concatenate([jnp.exp(jnp.concatenate([tot2[:, LANE * j:LANE * (j + 1)], pad], axis=0).T)
                             for j in range(width // LANE)], axis=0)

    nb = ngroup * nchunk
    shift = n.bit_length() - 1
    head3 = _iota2((nb, L, gw), 2) >> shift
    bd = functools.partial(block_diag, head_lane=head3)
    row = _iota2((nb, 2 * L, gw), 1)
    col = _iota2((nb, 2 * L, gw), 2) & (n - 1)
    causal = row - col >= jnp.where(row < L, 1, L)
    eye = (_iota2((nb, L, gw), 1) == (_iota2((nb, L, gw), 2) & (n - 1))).astype(F32)

    ar = jnp.concatenate([at, rt], axis=1)
    pb = jnp.where(causal, _bmm_nt(ar, bd(bt)), 0.0)
    pk = jnp.where(causal, _bmm_nt(ar, bd(kt)), 0.0)
    wv = _bmm(pk, bd(vb))
    mab = pb[:, :L]
    tinv = eye + mab
    pw = _bmm(mab, bd(mab))
    s = 4
    while s < L:
        both = _bmm(jnp.concatenate([tinv, pw], axis=1), bd(pw))
        tinv = tinv + both[:, :L]
        pw = both[:, L:]
        s *= 2
    tinv = tinv + _bmm(tinv, bd(pw))
    abar = _bmm(tinv, bd(at))
    u0 = _bmm(tinv, bd(wv[:, :L]))
    def same_head_blocks(x):
        out = x[:, :n]
        for hh in range(1, RWKV_PACK):
            out = jnp.where(head3 == hh, x[:, n * hh:n * (hh + 1)], out)
        return out

    wmat = same_head_blocks(_bmm_tn(bend, abar))
    nmat = same_head_blocks(_bmm_tn(jnp.concatenate([bend, kend], axis=1), jnp.concatenate([u0, vb], axis=1)))
    war = jnp.concatenate([wmat, abar, rt], axis=1).astype(BF16)
    nrb = pb[:, L:].astype(BF16)
    y0 = wv[:, L:]

    gs = range(ngroup)
    head2 = _iota2((n, gw), 1) >> shift
    st = [state_ref[j] for j in gs]
    for c in range(nchunk):
        idx = [j * nchunk + c for j in gs]
        rd = [_dot(war[idx[j]], block_diag(st[j], head2)) for j in gs]
        for j in gs:
            u = rd[j][n:n + L] + u0[idx[j]]
            y_ref[L * c:L * (c + 1), gw * j:gw * (j + 1)] = (
                rd[j][n + L:] + _dot(nrb[idx[j]], block_diag(u, head2)) + y0[idx[j]])
            decay = jnp.broadcast_to(gam_t[gw * j:gw * j + n, c:c + 1], (n, gw))
            for hh in range(1, RWKV_PACK):
                lo = gw * j + n * hh
                decay = jnp.where(head2 == hh, gam_t[lo:lo + n, c:c + 1], decay)
            st[j] = st[j] * decay + rd[j][:n] + nmat[idx[j]]
    for j in gs:
        state_ref[j] = st[j]

    y = y_ref[...]
    mean = _head_sum(y) * (1.0 / n)
    var = _head_sum(jnp.square(y - mean)) * (1.0 / n)
    y = (y - mean) * lax.rsqrt(var + RWKV_GN_EPS)
    y = y * lng_ref[...] + lnb_ref[...]
    y = y + _head_sum(r_ref[...] * k_ref[...] * rk_ref[...]) * v_ref[...]
    o_ref[...] = (y * g_ref[...]).astype(BF16)


def _rwkv_branch(x2, bsz, seq, gmix, wp, v_first):
    n_tok = x2.shape[0]
    tm = min(TOK_TILE, seq)
    ns = seq // tm
    has_vres = v_first is not None
    d = RWKV_DIM
    tok = lambda w: pl.BlockSpec((tm, w), lambda b, i: (b * ns + i, 0))
    vec = _full((1, d))
    in_specs = [tok(D_MODEL), _full((1, D_MODEL)), _full((D_MODEL, RWKV_IN)), _full((1, RWKV_IN)), vec,
                _full((LANE, d)), vec, _full((LANE, d)), _full((RWKV_G_RANK, d)), vec, vec]
    args = [x2, gmix, wp["w_rwkv"], wp["mu"], wp["w0"], wp["w2"], wp["a0"], wp["a2"], wp["g2"], wp["k_k"], wp["k_a"]]
    if has_vres:
        in_specs += [tok(d), vec, _full((d, LANE)), _full((LANE, d))]
        args += [v_first, wp["v0"], wp["v1"], wp["v2"]]
    r, lw, k, v, kk, a, g = pl.pallas_call(
        functools.partial(_rwkv_prep_kernel, has_vres),
        grid=(bsz, ns),
        in_specs=in_specs,
        out_specs=[tok(d)] * 7,
        out_shape=[jax.ShapeDtypeStruct((n_tok, d), F32)] * 7,
        scratch_shapes=[pltpu.VMEM((1, RWKV_IN), F32)],
        compiler_params=_params("parallel", "arbitrary"),
        name="rwkv_prep",
    )(*args)

    ts = min(RWKV_TILE, seq)
    nt = seq // ts
    width = RWKV_SCAN_HEADS * RWKV_HEAD
    blk = pl.BlockSpec((ts, width), lambda b, hp, i: (b * nt + i, hp))
    par = pl.BlockSpec((1, width), lambda b, hp, i: (0, hp))
    o = pl.pallas_call(
        _rwkv_scan_kernel,
        grid=(bsz, RWKV_HEADS // RWKV_SCAN_HEADS, nt),
        in_specs=[blk] * 7 + [par] * 3,
        out_specs=blk,
        out_shape=jax.ShapeDtypeStruct((n_tok, d), BF16),
        scratch_shapes=[pltpu.VMEM((RWKV_SCAN_HEADS // RWKV_PACK, RWKV_HEAD, RWKV_PACK * RWKV_HEAD), F32),
                        pltpu.VMEM((ts, width), F32)],
        compiler_params=_params("parallel", "parallel", "arbitrary"),
        name="rwkv_scan",
    )(r, lw, k, v, kk, a, g, wp["r_k"], wp["ln_g"], wp["ln_b"])
    return o, v


def _ssm_prep_kernel(x_ref, gmix_ref, w_ref, cw_ref, cb_ref, dtb_ref,
                     z_out, xs_out, b_out, c_out, dt_out, carry_ref):
    @pl.when(pl.program_id(1) == 0)
    def _():
        carry_ref[...] = jnp.zeros_like(carry_ref)

    h = _rms(x_ref[...], gmix_ref[...], NORM_EPS).astype(BF16)
    tm = h.shape[0]
    gw = SSM_CONV_GROUP
    top_row = _iota2((SUBLANE, gw), 0)
    for j in range(SSM_CONV_DIM // gw):
        cs = slice(gw * j, gw * (j + 1))
        xbc = _dot(h, w_ref[:, SSM_DIM + gw * j:SSM_DIM + gw * (j + 1)])
        if gw * j < SSM_DIM:
            z_out[:, cs] = _dot(h, w_ref[:, cs])
        else:
            dt_out[...] = _softplus(_dot(h, w_ref[:, SSM_DIM + SSM_CONV_DIM:]) + dtb_ref[...])
        prev = carry_ref[:, cs]
        carry_ref[:, cs] = xbc[tm - SUBLANE:, :]
        acc = xbc * cw_ref[SSM_CONV - 1:SSM_CONV, cs] + cb_ref[:, cs]
        for s in range(1, SSM_CONV):
            rolled = pltpu.roll(xbc, s, 0)
            top = jnp.where(top_row < s, pltpu.roll(prev, s, 0), rolled[:SUBLANE])
            shifted = jnp.concatenate([top, rolled[SUBLANE:]], axis=0)
            acc = acc + shifted * cw_ref[SSM_CONV - 1 - s:SSM_CONV - s, cs]
        act = acc * _sigmoid(acc)
        if gw * j < SSM_DIM:
            xs_out[:, cs] = act
        else:
            gn = SSM_GROUPS * SSM_STATE
            b_out[...] = act[:, :gn].astype(BF16)
            c_out[...] = act[:, gn:].astype(BF16)


def _ssd_kernel(xs_ref, b_ref, c_ref, dt_ref, z_ref, alog_ref, d_ref, ng_ref, o_ref, state_ref, y_ref):
    @pl.when(pl.program_id(1) == 0)
    def _():
        state_ref[...] = jnp.zeros_like(state_ref)

    L = xs_ref.shape[0]
    row = _iota2((L, L), 0)
    col = _iota2((L, L), 1)
    lower = row >= col
    tril = lower.astype(F32)
    eye = (_iota2((LANE, LANE), 0) == _iota2((LANE, LANE), 1)).astype(F32)

    dt = dt_ref[...]
    da = dt * (-jnp.exp(alog_ref[...]))
    acs = _dot_exact(tril, da)
    acs_t = _dot_nt_exact(eye, acs)
    dt_t = _dot_nt_exact(eye, dt)
    first = _iota2((1, LANE), 1) < SSM_HEAD

    for g in range(SSM_GROUPS):
        gs = slice(SSM_STATE * g, SSM_STATE * (g + 1))
        bm = b_ref[:, gs]
        cm = c_ref[:, gs]
        cb16 = _dot_nt(cm, bm).astype(BF16)
        bm_t = _dot_nt(eye, bm)
        for e in range(0, SSM_HPG, 2):
            pair = (SSM_HPG * g + e) // 2
            ps = slice(LANE * pair, LANE * (pair + 1))
            xs = xs_ref[:, ps]
            xs16 = xs.astype(BF16)
            st = state_ref[pair]
            y_in = _dot(cm, st)
            ys, sts = [], []
            for hd in (2 * pair, 2 * pair + 1):
                a_col = jnp.broadcast_to(acs[:, hd:hd + 1], (L, LANE))
                a_row = acs_t[hd:hd + 1, :]
                dt_row = dt_t[hd:hd + 1, :]
                tot = acs[L - 1:L, hd:hd + 1]
                seg = jnp.concatenate([jnp.exp(a_col - a_row[:, LANE * j:LANE * (j + 1)])
                                       for j in range(L // LANE)], axis=1)
                w = cb16 * jnp.where(lower, seg, 0.0).astype(BF16) * dt_row.astype(BF16)
                ys.append(_dot(w, xs16) + y_in * jnp.exp(a_col))
                sts.append(st * jnp.exp(tot) + _dot(bm_t * (dt_row * jnp.exp(tot - a_row)), xs16))
            y_ref[:, ps] = jnp.where(first, ys[0], ys[1]) + xs * d_ref[:, ps]
            state_ref[pair] = jnp.where(first, sts[0], sts[1])

    z = z_ref[...]
    y = y_ref[...] * (z * _sigmoid(z))
    gw = SSM_DIM // SSM_GROUPS
    for g in range(SSM_GROUPS):
        gs = slice(gw * g, gw * (g + 1))
        yg = y[:, gs]
        yg = yg * lax.rsqrt(jnp.mean(yg * yg, axis=-1, keepdims=True) + SSM_NORM_EPS)
        o_ref[:, gs] = (yg * ng_ref[:, gs]).astype(BF16)


def _ssm_branch(x2, bsz, seq, gmix, wp):
    n_tok = x2.shape[0]
    tm = min(TOK_TILE, seq)
    ns = seq // tm
    gn = SSM_GROUPS * SSM_STATE
    tok = lambda w: pl.BlockSpec((tm, w), lambda b, i: (b * ns + i, 0))
    z, xs, bm, cm, dt = pl.pallas_call(
        _ssm_prep_kernel,
        grid=(bsz, ns),
        in_specs=[tok(D_MODEL), _full((1, D_MODEL)), _full(wp["w_ssm"].shape), _full((SSM_CONV, SSM_CONV_DIM)),
                  _full((1, SSM_CONV_DIM)), _full((1, LANE))],
        out_specs=[tok(SSM_DIM), tok(SSM_DIM), tok(gn), tok(gn), tok(LANE)],
        out_shape=[jax.ShapeDtypeStruct((n_tok, SSM_DIM), F32), jax.ShapeDtypeStruct((n_tok, SSM_DIM), F32),
                   jax.ShapeDtypeStruct((n_tok, gn), BF16), jax.ShapeDtypeStruct((n_tok, gn), BF16),
                   jax.ShapeDtypeStruct((n_tok, LANE), F32)],
        scratch_shapes=[pltpu.VMEM((SUBLANE, SSM_CONV_DIM), F32)],
        compiler_params=_params("parallel", "arbitrary"),
        name="ssm_prep",
    )(x2, gmix, wp["w_ssm"], wp["conv_w"], wp["conv_b"], wp["dt_bias"])

    assert seq % SSM_CHUNK == 0
    nc = seq // SSM_CHUNK
    ch = lambda w: pl.BlockSpec((SSM_CHUNK, w), lambda b, c: (b * nc + c, 0))
    return pl.pallas_call(
        _ssd_kernel,
        grid=(bsz, nc),
        in_specs=[ch(SSM_DIM), ch(gn), ch(gn), ch(LANE), ch(SSM_DIM), _full((1, LANE)), _full((1, SSM_DIM)),
                  _full((1, SSM_DIM))],
        out_specs=ch(SSM_DIM),
        out_shape=jax.ShapeDtypeStruct((n_tok, SSM_DIM), BF16),
        scratch_shapes=[pltpu.VMEM((SSM_HEADS // 2, SSM_STATE, 2 * SSM_HEAD), F32),
                        pltpu.VMEM((SSM_CHUNK, SSM_DIM), F32)],
        compiler_params=_params("parallel", "arbitrary"),
        name="ssd",
    )(xs, bm, cm, dt, z, wp["a_log"], wp["d_skip"], wp["norm_g"])


def _merge_kernel(x_ref, gmix_ref, wg_ref, om_ref, or_ref, os_ref, wm_ref, wr_ref, ws_ref, wo_ref, o_ref):
    x = x_ref[...]
    h = _rms(x, gmix_ref[...], NORM_EPS).astype(BF16)
    d = D_MODEL
    merged = _sigmoid(_dot(h, wg_ref[:, :d])) * _dot(om_ref[...], wm_ref[...])
    merged = merged + _sigmoid(_dot(h, wg_ref[:, d:2 * d])) * _dot(or_ref[...], wr_ref[...])
    merged = merged + _sigmoid(_dot(h, wg_ref[:, 2 * d:])) * _dot(os_ref[...], ws_ref[...])
    o_ref[...] = x + _dot(merged, wo_ref[...])


def _ffn_kernel(x_ref, g_ref, w1_ref, w2_ref, o_ref, h_ref):
    j = pl.program_id(1)

    @pl.when(j == 0)
    def _():
        x = x_ref[...]
        h_ref[...] = _rms(x, g_ref[...], NORM_EPS).astype(BF16)
        o_ref[...] = x

    u = jnp.maximum(_dot(h_ref[...], w1_ref[...]), 0.0)
    o_ref[...] += _dot(u * u, w2_ref[...])


def _merge_ffn(x2, gmix, wp, o_mla, o_rwkv, o_ssm):
    n_tok = x2.shape[0]
    tm = min(TOK_TILE, n_tok)
    tok = lambda w: pl.BlockSpec((tm, w), lambda i: (i, 0))
    x2 = pl.pallas_call(
        _merge_kernel,
        grid=(n_tok // tm,),
        in_specs=[tok(D_MODEL), _full((1, D_MODEL)), _full((D_MODEL, GATE_IN)), tok(o_mla.shape[1]),
                  tok(o_rwkv.shape[1]), tok(o_ssm.shape[1]), _full(wp["w_br_mla"].shape),
                  _full(wp["w_br_rwkv"].shape), _full(wp["w_br_ssm"].shape), _full((D_MODEL, D_MODEL))],
        out_specs=tok(D_MODEL),
        out_shape=jax.ShapeDtypeStruct((n_tok, D_MODEL), F32),
        compiler_params=_params("parallel"),
        name="merge",
    )(x2, gmix, wp["w_gate"], o_mla, o_rwkv, o_ssm, wp["w_br_mla"], wp["w_br_rwkv"], wp["w_br_ssm"], wp["w_out"])

    tf = min(FFN_TOK_TILE, n_tok)
    ff = FFN_FF_TILE
    return pl.pallas_call(
        _ffn_kernel,
        grid=(n_tok // tf, D_FF // ff),
        in_specs=[pl.BlockSpec((tf, D_MODEL), lambda i, j: (i, 0)), _full((1, D_MODEL)),
                  pl.BlockSpec((D_MODEL, ff), lambda i, j: (0, j)), pl.BlockSpec((ff, D_MODEL), lambda i, j: (j, 0))],
        out_specs=pl.BlockSpec((tf, D_MODEL), lambda i, j: (i, 0)),
        out_shape=jax.ShapeDtypeStruct((n_tok, D_MODEL), F32),
        scratch_shapes=[pltpu.VMEM((tf, D_MODEL), BF16)],
        compiler_params=_params("parallel", "arbitrary"),
        name="ffn",
    )(x2, wp["norm_ffn_g"], wp["w_ff1"], wp["w_ff2"])


def _pad_cols(w, width):
    return jnp.pad(w, ((0, 0), (0, width - w.shape[1])))


def _pad_rows(w, height, before=0):
    return jnp.pad(w, ((before, height - before - w.shape[0]), (0, 0)))


def _row(v, width=None):
    v = v.reshape(1, -1).astype(F32)
    return v if width is None else _pad_cols(v, width)


def _layer_weights(l, p):
    w_in = p["w_in"][l]
    o_r = MLA_IN
    o_s = o_r + RWKV_IN
    o_g = o_s + SSM_IN
    zc = lambda n: jnp.zeros((D_MODEL, n), w_in.dtype)
    w_mla = jnp.concatenate([w_in[:, :MLA_Q_RANK + MLA_KV_RANK], zc(MLA_NOPE), w_in[:, MLA_Q_RANK + MLA_KV_RANK:MLA_IN],
                             zc(LANE - MLA_QK)], axis=1)
    half = MLA_ROPE // 2

    def slot(t, rotate):
        lo, hi = t[..., MLA_NOPE:MLA_NOPE + half], t[..., MLA_NOPE + half:]
        parts = [jnp.zeros_like(t[..., :MLA_NOPE]), -hi if rotate == "signed" else hi, lo] if rotate else [t]
        t = jnp.concatenate(parts, axis=-1)
        return jnp.pad(t, [(0, 0)] * (t.ndim - 1) + [(0, LANE - MLA_QK)])

    w_uq = p["mla_w_uq"][l].reshape(MLA_Q_RANK, MLA_HEADS, MLA_QK)
    w_uq = jnp.concatenate([slot(w_uq, None).reshape(MLA_Q_RANK, MLA_HEADS * LANE),
                            slot(w_uq, "signed").reshape(MLA_Q_RANK, MLA_HEADS * LANE)], axis=1)
    q_head_g = p["mla_q_head_g"][l].astype(F32)
    q_head_g = jnp.stack([slot(q_head_g, None), slot(q_head_g, "moved")])
    w_ukv = p["mla_w_ukv"][l].reshape(MLA_KV_RANK, MLA_HEADS, MLA_NOPE + MLA_V)
    w_uk = jnp.pad(w_ukv[:, :, :MLA_NOPE], ((0, 0), (0, 0), (0, LANE - MLA_NOPE))).reshape(MLA_KV_RANK, MLA_HEADS * LANE)
    w_uv = jnp.pad(w_ukv[:, :, MLA_NOPE:], ((0, 0), (0, 0), (0, LANE - MLA_V))).reshape(MLA_KV_RANK, MLA_HEADS * LANE)
    w_ssm = _pad_cols(w_in[:, o_s:o_g], SSM_DIM + SSM_CONV_DIM + LANE)
    wp = {
        "w_mla": w_mla.astype(BF16), "w_uq": w_uq.astype(BF16), "w_uk": w_uk.astype(BF16), "w_uv": w_uv.astype(BF16),
        "q_norm_g": _row(p["mla_q_norm_g"][l]), "kv_norm_g": _row(p["mla_kv_norm_g"][l]),
        "q_head_g": q_head_g, "k_head_g": _row(p["mla_k_head_g"][l], LANE),
        "w_rwkv": w_in[:, o_r:o_s].astype(BF16), "mu": _row(p["rwkv_mu"][l]), "w0": _row(p["rwkv_w0"][l]),
        "w2": _pad_rows(p["rwkv_w2"][l], LANE).astype(BF16), "a0": _row(p["rwkv_a0"][l]),
        "a2": _pad_rows(p["rwkv_a2"][l], LANE, before=RWKV_W_RANK).astype(BF16), "g2": p["rwkv_g2"][l].astype(BF16),
        "k_k": _row(p["rwkv_k_k"][l]), "k_a": _row(p["rwkv_k_a"][l]), "r_k": _row(p["rwkv_r_k"][l]),
        "ln_g": _row(p["rwkv_ln_g"][l]), "ln_b": _row(p["rwkv_ln_b"][l]),
        "w_ssm": w_ssm.astype(BF16), "conv_w": p["ssm_conv_w"][l].astype(F32), "conv_b": _row(p["ssm_conv_b"][l]),
        "dt_bias": _row(p["ssm_dt_bias"][l], LANE), "a_log": _row(p["ssm_a_log"][l], LANE),
        "d_skip": _row(jnp.repeat(p["ssm_d"][l], SSM_HEAD)), "norm_g": _row(p["ssm_norm_g"][l]),
        "w_gate": w_in[:, o_g:].astype(BF16), "w_br_mla": p["w_br_mla"][l].astype(BF16),
        "w_br_rwkv": p["w_br_rwkv"][l].astype(BF16), "w_br_ssm": p["w_br_ssm"][l].astype(BF16),
        "w_out": p["w_out"][l].astype(BF16), "norm_ffn_g": _row(p["norm_ffn_g"][l]),
        "w_ff1": p["w_ff1"][l].astype(BF16), "w_ff2": p["w_ff2"][l].astype(BF16),
    }
    if l > 0:
        wp["v0"] = _row(p["rwkv_v0"][l - 1])
        wp["v1"] = _pad_cols(p["rwkv_v1"][l - 1], LANE).astype(BF16)
        wp["v2"] = _pad_rows(p["rwkv_v2"][l - 1], LANE).astype(BF16)
    return wp


def kernel(x, positions, norm_mix_g, w_in, mla_q_norm_g, mla_kv_norm_g, mla_w_uq, mla_w_ukv, mla_q_head_g, mla_k_head_g, rwkv_mu, rwkv_w0, rwkv_w2, rwkv_a0, rwkv_a2, rwkv_g2, rwkv_v0, rwkv_v1, rwkv_v2, rwkv_k_k, rwkv_k_a, rwkv_r_k, rwkv_ln_g, rwkv_ln_b, ssm_conv_w, ssm_conv_b, ssm_dt_bias, ssm_a_log, ssm_d, ssm_norm_g, w_br_mla, w_br_rwkv, w_br_ssm, w_out, norm_ffn_g, w_ff1, w_ff2):
    p = dict(w_in=w_in, mla_q_norm_g=mla_q_norm_g, mla_kv_norm_g=mla_kv_norm_g, mla_w_uq=mla_w_uq,
             mla_w_ukv=mla_w_ukv, mla_q_head_g=mla_q_head_g, mla_k_head_g=mla_k_head_g, rwkv_mu=rwkv_mu,
             rwkv_w0=rwkv_w0, rwkv_w2=rwkv_w2, rwkv_a0=rwkv_a0, rwkv_a2=rwkv_a2, rwkv_g2=rwkv_g2, rwkv_v0=rwkv_v0,
             rwkv_v1=rwkv_v1, rwkv_v2=rwkv_v2, rwkv_k_k=rwkv_k_k, rwkv_k_a=rwkv_k_a, rwkv_r_k=rwkv_r_k,
             rwkv_ln_g=rwkv_ln_g, rwkv_ln_b=rwkv_ln_b, ssm_conv_w=ssm_conv_w, ssm_conv_b=ssm_conv_b,
             ssm_dt_bias=ssm_dt_bias, ssm_a_log=ssm_a_log, ssm_d=ssm_d, ssm_norm_g=ssm_norm_g, w_br_mla=w_br_mla,
             w_br_rwkv=w_br_rwkv, w_br_ssm=w_br_ssm, w_out=w_out, norm_ffn_g=norm_ffn_g, w_ff1=w_ff1, w_ff2=w_ff2)
    bsz, seq, d_model = x.shape
    depth = w_in.shape[0]
    x2 = x.reshape(bsz * seq, d_model)
    tables = _rope_tables(positions)
    v_first = None
    for l in range(depth):
        wp = _layer_weights(l, p)
        gmix = _row(norm_mix_g[l])
        o_mla = _mla_branch(x2, bsz, seq, gmix, wp, tables)
        o_rwkv, v = _rwkv_branch(x2, bsz, seq, gmix, wp, v_first)
        if l == 0:
            v_first = v
        o_ssm = _ssm_branch(x2, bsz, seq, gmix, wp)
        x2 = _merge_ffn(x2, gmix, wp, o_mla, o_rwkv, o_ssm)
    return x2.reshape(bsz, seq, d_model)
```

```python
import functools

import jax
import jax.numpy as jnp
from jax import lax
from jax.experimental import pallas as pl
from jax.experimental.pallas import tpu as pltpu

F32 = jnp.float32
BF16 = jnp.bfloat16

D_MODEL = 1024
MLA_HEADS = 8
MLA_NOPE = 64
MLA_ROPE = 32
MLA_QK = MLA_NOPE + MLA_ROPE
MLA_V = 64
MLA_Q_RANK = 384
MLA_KV_RANK = 256
ROPE_THETA = 10000.0
RWKV_HEAD = 64
RWKV_HEADS = 8
RWKV_DIM = RWKV_HEADS * RWKV_HEAD
RWKV_W_RANK = 64
RWKV_A_RANK = 64
RWKV_V_RANK = 32
RWKV_G_RANK = 128
RWKV_GN_EPS = 64e-5
SSM_HEAD = 64
SSM_HEADS = 16
SSM_DIM = SSM_HEADS * SSM_HEAD
SSM_GROUPS = 2
SSM_HPG = SSM_HEADS // SSM_GROUPS
SSM_STATE = 128
SSM_CONV = 4
SSM_CHUNK = 256
SSM_CONV_DIM = SSM_DIM + 2 * SSM_GROUPS * SSM_STATE
SSM_NORM_EPS = 1e-5
N_BRANCH = 3
D_FF = 4 * D_MODEL
NORM_EPS = 1e-6
MLA_IN = MLA_Q_RANK + MLA_KV_RANK + MLA_ROPE
RWKV_IN = 3 * RWKV_DIM + RWKV_W_RANK + RWKV_A_RANK + RWKV_G_RANK
SSM_IN = SSM_DIM + SSM_CONV_DIM + SSM_HEADS
GATE_IN = N_BRANCH * D_MODEL

LOG2_E = 1.4426950408889634
LANE = 128
SUBLANE = 8
VMEM_LIMIT = 56 * 1024 * 1024

TOK_TILE = 512
MLA_PREP_PARTS = 2
FFN_TOK_TILE = 1024
FFN_FF_TILE = 1024
ATTN_TILE = 512
ATTN_HEADS = 8
ATTN_KEYS = 256
SSM_CONV_GROUP = 512
RWKV_CHUNK = 64
RWKV_TILE = 512
RWKV_SCAN_HEADS = 8
RWKV_PACK = 4


def _dot(a, b):
    return jnp.dot(a.astype(BF16), b.astype(BF16), preferred_element_type=F32)


def _dot_nt(a, b):
    return lax.dot_general(a.astype(BF16), b.astype(BF16), (((1,), (1,)), ((), ())),
                           preferred_element_type=F32)


def _dot_tn(a, b):
    return lax.dot_general(a.astype(BF16), b.astype(BF16), (((0,), (0,)), ((), ())),
                           preferred_element_type=F32)


def _bmm(a, b):
    return lax.dot_general(a.astype(BF16), b.astype(BF16), (((2,), (1,)), ((0,), (0,))),
                           preferred_element_type=F32)


def _bmm_nt(a, b):
    return lax.dot_general(a.astype(BF16), b.astype(BF16), (((2,), (2,)), ((0,), (0,))),
                           preferred_element_type=F32)


def _bmm_tn(a, b):
    return lax.dot_general(a.astype(BF16), b.astype(BF16), (((1,), (1,)), ((0,), (0,))),
                           preferred_element_type=F32)


def _dot_exact(a, b):
    return jnp.dot(a, b, preferred_element_type=F32, precision=lax.Precision.HIGHEST)


def _dot_nt_exact(a, b):
    return lax.dot_general(a, b, (((1,), (1,)), ((), ())), preferred_element_type=F32,
                           precision=lax.Precision.HIGHEST)


def _rms(x, g, eps):
    return x * lax.rsqrt(jnp.mean(x * x, axis=-1, keepdims=True) + eps) * g


def _sigmoid(x):
    return 1.0 / (1.0 + jnp.exp(-x))


def _softplus(x):
    return jnp.maximum(x, 0.0) + jnp.log(1.0 + jnp.exp(-jnp.abs(x)))


def _iota2(shape, dim):
    return lax.broadcasted_iota(jnp.int32, shape, dim)


def _params(*sem):
    return pltpu.CompilerParams(dimension_semantics=sem, vmem_limit_bytes=VMEM_LIMIT)


def _full(shape):
    return pl.BlockSpec(shape, lambda *_: (0,) * len(shape))


def _rope_table_kernel(pos_ref, freq_ref, cos_ref, sin_ref):
    ang = pos_ref[...].astype(F32) * freq_ref[...]
    cos_ref[...] = jnp.cos(ang)
    sin_ref[...] = jnp.sin(ang)


def _rope_tables(positions):
    half = MLA_ROPE // 2
    n_tok = positions.size
    inv_freq = ROPE_THETA ** (-jnp.arange(half, dtype=jnp.float32) / half)
    per_row = LANE // half
    rows = n_tok // per_row
    pos = jnp.broadcast_to(positions.reshape(rows, per_row, 1), (rows, per_row, half)).reshape(rows, LANE)
    freq = jnp.tile(inv_freq, per_row).reshape(1, LANE)
    blk = min(rows, 512)
    cos, sin = pl.pallas_call(
        _rope_table_kernel,
        grid=(rows // blk,),
        in_specs=[pl.BlockSpec((blk, LANE), lambda i: (i, 0)), _full((1, LANE))],
        out_specs=[pl.BlockSpec((blk, LANE), lambda i: (i, 0))] * 2,
        out_shape=[jax.ShapeDtypeStruct((rows, LANE), F32)] * 2,
        compiler_params=_params("parallel"),
        name="rope_table",
    )(pos, freq)
    cos = cos.reshape(n_tok, half)
    sin = sin.reshape(n_tok, half)
    one = jnp.ones((n_tok, MLA_NOPE), F32)
    z_nope = jnp.zeros((n_tok, MLA_NOPE), F32)
    z_half = jnp.zeros((n_tok, half), F32)
    z_pad = jnp.zeros((n_tok, LANE - MLA_QK), F32)
    cosf = jnp.concatenate([one, cos, cos, z_pad], axis=1)
    sina = jnp.concatenate([z_nope, -sin, z_half, z_pad], axis=1)
    sinb = jnp.concatenate([z_nope, z_half, sin, z_pad], axis=1)
    return cosf, sina, sinb


def _mla_prep_kernel(x_ref, gmix_ref, wmla_ref, qg_ref, kvg_ref, wuq_ref, wuk_ref, wuv_ref,
                     qhg_ref, khg_ref, cos_ref, sina_ref, sinb_ref, q_ref, k_ref, v_ref):
    hq = MLA_HEADS * LANE
    tm = x_ref.shape[0]
    parts = [slice(tm * i // MLA_PREP_PARTS, tm * (i + 1) // MLA_PREP_PARTS) for i in range(MLA_PREP_PARTS)]
    one_lane = (_iota2((1, hq), 1) & (LANE - 1)) == MLA_V
    khg = khg_ref[...]
    scale = MLA_QK ** -0.5 * LOG2_E

    def inv_rms(t):
        return lax.rsqrt(jnp.sum(t * t, axis=-1, keepdims=True) * (1.0 / MLA_QK) + NORM_EPS)

    latent = [_dot(_rms(x_ref[rs, :], gmix_ref[...], NORM_EPS), wmla_ref[...])
              for rs in parts]
    up = []
    for rs, c in zip(parts, latent):
        cq = _rms(c[:, :MLA_Q_RANK], qg_ref[...], NORM_EPS)
        ckv = _rms(c[:, MLA_Q_RANK:MLA_Q_RANK + MLA_KV_RANK], kvg_ref[...], NORM_EPS)
        qall = _dot(cq, wuq_ref[...])
        kf = _dot(ckv, wuk_ref[...])
        v_ref[rs, :] = jnp.where(one_lane, 1.0, _dot(ckv, wuv_ref[...])).astype(BF16)
        up.append((qall, kf))
    for rs, c, (qall, kf) in zip(parts, latent, up):
        krope = c[:, MLA_Q_RANK + MLA_KV_RANK:]
        cosf, sina, sinb = cos_ref[rs, :], sina_ref[rs, :], sinb_ref[rs, :]
        q_cos = qhg_ref[0:1, :] * cosf
        q_sin = qhg_ref[1:2, :] * (sinb - sina)
        k_cos = khg * cosf
        kg = krope * khg
        k_rot = pltpu.roll(kg, LANE - MLA_ROPE // 2, 1) * sina + pltpu.roll(kg, MLA_ROPE // 2, 1) * sinb
        for hd in range(MLA_HEADS):
            sl = slice(LANE * hd, LANE * (hd + 1))
            qh = qall[:, sl]
            qr = qall[:, hq + LANE * hd:hq + LANE * (hd + 1)]
            q_ref[rs, sl] = ((qh * q_cos + qr * q_sin) * (inv_rms(qh) * scale)).astype(BF16)
            kh = kf[:, sl] + krope
            k_ref[rs, sl] = ((kh * k_cos + k_rot) * inv_rms(kh)).astype(BF16)


def _attn_kernel(q_ref, k_ref, v_ref, o_ref, m_ref, acc_ref):
    i = pl.program_id(2)
    tq = q_ref.shape[0]
    heads = range(ATTN_HEADS)
    acc_rows = acc_ref.shape[1]
    m_ref[...] = jnp.full(m_ref.shape, -jnp.inf, F32)
    acc_ref[...] = jnp.zeros(acc_ref.shape, F32)

    def step(j, masked):
        for sub in range(tq // ATTN_KEYS):
            substep(j * tq + sub * ATTN_KEYS, sub * ATTN_KEYS if masked else 0, masked)

    def substep(start, q0, masked):
        rows = pl.ds(pl.multiple_of(start, ATTN_KEYS), ATTN_KEYS)
        qs = slice(q0, tq)
        scores = [_dot_nt(k_ref[rows, LANE * hh:LANE * (hh + 1)], q_ref[qs, LANE * hh:LANE * (hh + 1)])
                  for hh in heads]
        probs = []
        for hh in heads:
            s = scores[hh]
            if masked:
                s = jnp.where(_iota2(s.shape, 1) >= _iota2(s.shape, 0), s, -jnp.inf)
            m = m_ref[hh, :, qs]
            m_new = jnp.maximum(m, jnp.max(s, axis=0, keepdims=True))
            p = jnp.exp2(s - m_new)
            alpha = jnp.exp2(m - m_new)
            m_ref[hh, :, qs] = m_new
            probs.append((p.astype(BF16), alpha))
        for hh in heads:
            p, alpha = probs[hh]
            v = v_ref[rows, LANE * hh:LANE * (hh + 1)]
            acc_ref[hh, :, qs] = alpha * acc_ref[hh, :, qs] + _dot_tn(v, p)[:acc_rows]

    def body(j, carry):
        step(j, False)
        return carry

    lax.fori_loop(0, i, body, 0)
    step(i, True)
    out = jnp.concatenate([acc_ref[hh, :MLA_V, :] / acc_ref[hh, MLA_V:MLA_V + 1, :] for hh in heads],
                          axis=0)
    o_ref[...] = out.T.astype(BF16)


def _mla_branch(x2, bsz, seq, gmix, wp, tables):
    n_tok = x2.shape[0]
    tm = min(TOK_TILE, seq)
    cosf, sina, sinb = tables
    tok = lambda w: pl.BlockSpec((tm, w), lambda i: (i, 0))
    hq = MLA_HEADS * LANE
    q, k, v = pl.pallas_call(
        _mla_prep_kernel,
        grid=(n_tok // tm,),
        in_specs=[tok(D_MODEL), _full((1, D_MODEL)), _full(wp["w_mla"].shape), _full((1, MLA_Q_RANK)),
                  _full((1, MLA_KV_RANK)), _full(wp["w_uq"].shape), _full(wp["w_uk"].shape),
                  _full(wp["w_uv"].shape), _full((2, LANE)), _full((1, LANE)), tok(LANE), tok(LANE), tok(LANE)],
        out_specs=[tok(hq)] * 3,
        out_shape=[jax.ShapeDtypeStruct((n_tok, hq), BF16)] * 3,
        compiler_params=_params("parallel"),
        name="mla_prep",
    )(x2, gmix, wp["w_mla"], wp["q_norm_g"], wp["kv_norm_g"], wp["w_uq"], wp["w_uk"], wp["w_uv"],
      wp["q_head_g"], wp["k_head_g"], cosf, sina, sinb)

    tq = min(ATTN_TILE, seq)
    nq = seq // tq
    return pl.pallas_call(
        _attn_kernel,
        grid=(bsz, MLA_HEADS // ATTN_HEADS, nq),
        in_specs=[pl.BlockSpec((tq, ATTN_HEADS * LANE), lambda b, hp, i: (b * nq + i, hp)),
                  pl.BlockSpec((seq, ATTN_HEADS * LANE), lambda b, hp, i: (b, hp)),
                  pl.BlockSpec((seq, ATTN_HEADS * LANE), lambda b, hp, i: (b, hp))],
        out_specs=pl.BlockSpec((tq, ATTN_HEADS * MLA_V), lambda b, hp, i: (b * nq + i, hp)),
        out_shape=jax.ShapeDtypeStruct((n_tok, MLA_HEADS * MLA_V), BF16),
        scratch_shapes=[pltpu.VMEM((ATTN_HEADS, 1, tq), F32),
                        pltpu.VMEM((ATTN_HEADS, MLA_V + SUBLANE, tq), F32)],
        compiler_params=_params("parallel", "parallel", "arbitrary"),
        name="mla_attention",
    )(q, k, v)


def _rwkv_prep_kernel(has_vres, *refs):
    if has_vres:
        (x_ref, gmix_ref, w_ref, mu_ref, w0_ref, w2_ref, a0_ref, a2_ref, g2_ref,
         vfirst_ref, v0_ref, v1_ref, v2_ref,
         r_out, lw_out, k_out, v_out, a_out, g_out, carry_ref) = refs
    else:
        (x_ref, gmix_ref, w_ref, mu_ref, w0_ref, w2_ref, a0_ref, a2_ref, g2_ref,
         r_out, lw_out, k_out, v_out, a_out, g_out, carry_ref) = refs

    @pl.when(pl.program_id(1) == 0)
    def _():
        carry_ref[...] = jnp.zeros_like(carry_ref)

    h = _rms(x_ref[...], gmix_ref[...], NORM_EPS).astype(BF16)
    tm = h.shape[0]
    d = RWKV_DIM

    def mixed(lo, hi):
        p = _dot(h, w_ref[:, lo:hi])
        prev = jnp.where(_iota2(p.shape, 0) == 0, carry_ref[:, lo:hi], pltpu.roll(p, 1, 0))
        carry_ref[:, lo:hi] = p[tm - 1:tm, :]
        return p + (prev - p) * mu_ref[:, lo:hi]

    lora = mixed(3 * d, RWKV_IN)
    xwa = lora[:, :LANE]
    r_out[...] = mixed(0, d)
    w = -_softplus(-(w0_ref[...] + _dot(jnp.tanh(xwa), w2_ref[...]))) - 0.5
    lw_out[...] = -jnp.exp(w)
    a = _sigmoid(a0_ref[...] + _dot(xwa, a2_ref[...]))
    a_out[...] = a
    g_out[...] = _dot(_sigmoid(lora[:, LANE:]), g2_ref[...])
    k_out[...] = mixed(d, 2 * d)
    v = mixed(2 * d, 3 * d)
    if has_vres:
        gate = _sigmoid(v0_ref[...] + _dot(_dot(v, v1_ref[...]), v2_ref[...]))
        v = v + (vfirst_ref[...] - v) * gate
    v_out[...] = v


def _head_sum(x):
    blocks = []
    for j in range(x.shape[-1] // LANE):
        xb = x[..., LANE * j:LANE * (j + 1)]
        first = _iota2(xb.shape, xb.ndim - 1) < RWKV_HEAD
        s0 = jnp.sum(jnp.where(first, xb, 0.0), axis=-1, keepdims=True)
        s1 = jnp.sum(jnp.where(first, 0.0, xb), axis=-1, keepdims=True)
        blocks.append(jnp.where(first, s0, s1))
    return blocks[0] if len(blocks) == 1 else jnp.concatenate(blocks, axis=-1)


def _rwkv_scan_kernel(r_ref, lw_ref, k_ref, v_ref, a_ref, g_ref, kkp_ref, kap_ref, rk_ref, lng_ref, lnb_ref,
                      o_ref, state_ref, y_ref):
    L = RWKV_CHUNK
    n = RWKV_HEAD
    ts, width = r_ref.shape
    nchunk = ts // L
    sh3 = (nchunk, L, width)

    @pl.when(pl.program_id(2) == 0)
    def _():
        state_ref[...] = jnp.zeros_like(state_ref)

    t_idx = _iota2(sh3, 1)
    r = r_ref[...].reshape(sh3)
    lw = lw_ref[...].reshape(sh3)
    v = v_ref[...].reshape(sh3)
    a = a_ref[...].reshape(sh3)
    k_in = k_ref[...].reshape(sh3)
    kk = k_in * kkp_ref[...]
    k = k_in * (1.0 + (a - 1.0) * kap_ref[...])

    kk = kk / jnp.maximum(jnp.sqrt(_head_sum(kk * kk)), 1e-12)
    cum = lw
    s = 1
    while s < L:
        cum = cum + jnp.where(t_idx >= s, pltpu.roll(cum, s, 1), 0.0)
        s *= 2
    tot = cum[:, L - 1:L, :]
    inv = jnp.exp(-cum)
    to_end = jnp.exp(tot - cum)

    gw = RWKV_PACK * n
    ngroup = width // gw

    def groups_to_batch(x):
        return jnp.concatenate([x[:, :, gw * j:gw * (j + 1)] for j in range(ngroup)], axis=0)

    def block_diag(x, head_lane):
        xb = x.astype(BF16)
        zero = jnp.zeros_like(xb)
        return jnp.concatenate([jnp.where(head_lane == hh, xb, zero) for hh in range(RWKV_PACK)], axis=-2)

    at = groups_to_batch(-kk * jnp.exp(cum - lw))
    rt = groups_to_batch(r * jnp.exp(cum))
    bt = groups_to_batch(kk * a * inv)
    kt = groups_to_batch(k * inv)
    bend = groups_to_batch(kk * a * to_end).astype(BF16)
    kend = groups_to_batch(k * to_end).astype(BF16)
    vb = groups_to_batch(v)
    tot2 = tot.reshape(nchunk, width)
    pad = jnp.zeros((LANE - nchunk, LANE), F32)
    gam_t = jnp.concatenate([jnp.exp(jnp.concatenate([tot2[:, LANE * j:LANE * (j + 1)], pad], axis=0).T)
                             for j in range(width // LANE)], axis=0)

    nb = ngroup * nchunk
    shift = n.bit_length() - 1
    head3 = _iota2((nb, L, gw), 2) >> shift
    bd = functools.partial(block_diag, head_lane=head3)
    row = _iota2((nb, 2 * L, gw), 1)
    col = _iota2((nb, 2 * L, gw), 2) & (n - 1)
    causal = row - col >= jnp.where(row < L, 1, L)
    eye = (_iota2((nb, L, gw), 1) == (_iota2((nb, L, gw), 2) & (n - 1))).astype(F32)

    ar = jnp.concatenate([at, rt], axis=1)
    pb = jnp.where(causal, _bmm_nt(ar, bd(bt)), 0.0)
    pk = jnp.where(causal, _bmm_nt(ar, bd(kt)), 0.0)
    wv = _bmm(pk, bd(vb))
    mab = pb[:, :L]
    tinv = eye + mab
    pw = _bmm(mab, bd(mab))
    s = 4
    while s < L:
        both = _bmm(jnp.concatenate([tinv, pw], axis=1), bd(pw))
        tinv = tinv + both[:, :L]
        pw = both[:, L:]
        s *= 2
    tinv = tinv + _bmm(tinv, bd(pw))
    abar = _bmm(tinv, bd(at))
    u0 = _bmm(tinv, bd(wv[:, :L]))
    def same_head_blocks(x):
        out = x[:, :n]
        for hh in range(1, RWKV_PACK):
            out = jnp.where(head3 == hh, x[:, n * hh:n * (hh + 1)], out)
        return out

    wmat = same_head_blocks(_bmm_tn(bend, abar))
    nmat = same_head_blocks(_bmm_tn(jnp.concatenate([bend, kend], axis=1), jnp.concatenate([u0, vb], axis=1)))
    war = jnp.concatenate([wmat, abar, rt], axis=1).astype(BF16)
    nrb = pb[:, L:].astype(BF16)
    y0 = wv[:, L:]

    gs = range(ngroup)
    head2 = _iota2((n, gw), 1) >> shift
    st = [state_ref[j] for j in gs]
    for c in range(nchunk):
        idx = [j * nchunk + c for j in gs]
        rd = [_dot(war[idx[j]], block_diag(st[j], head2)) for j in gs]
        for j in gs:
            u = rd[j][n:n + L] + u0[idx[j]]
            y_ref[L * c:L * (c + 1), gw * j:gw * (j + 1)] = (
                rd[j][n + L:] + _dot(nrb[idx[j]], block_diag(u, head2)) + y0[idx[j]])
            decay = jnp.broadcast_to(gam_t[gw * j:gw * j + n, c:c + 1], (n, gw))
            for hh in range(1, RWKV_PACK):
                lo = gw * j + n * hh
                decay = jnp.where(head2 == hh, gam_t[lo:lo + n, c:c + 1], decay)
            st[j] = st[j] * decay + rd[j][:n] + nmat[idx[j]]
    for j in gs:
        state_ref[j] = st[j]

    y = y_ref[...]
    mean = _head_sum(y) * (1.0 / n)
    var = _head_sum(jnp.square(y - mean)) * (1.0 / n)
    y = (y - mean) * lax.rsqrt(var + RWKV_GN_EPS)
    y = y * lng_ref[...] + lnb_ref[...]
    y = y + _head_sum(r_ref[...] * k.reshape(ts, width) * rk_ref[...]) * v_ref[...]
    o_ref[...] = (y * g_ref[...]).astype(BF16)


def _rwkv_branch(x2, bsz, seq, gmix, wp, v_first):
    n_tok = x2.shape[0]
    tm = min(TOK_TILE, seq)
    ns = seq // tm
    has_vres = v_first is not None
    d = RWKV_DIM
    tok = lambda w: pl.BlockSpec((tm, w), lambda b, i: (b * ns + i, 0))
    vec = _full((1, d))
    in_specs = [tok(D_MODEL), _full((1, D_MODEL)), _full((D_MODEL, RWKV_IN)), _full((1, RWKV_IN)), vec,
                _full((LANE, d)), vec, _full((LANE, d)), _full((RWKV_G_RANK, d))]
    args = [x2, gmix, wp["w_rwkv"], wp["mu"], wp["w0"], wp["w2"], wp["a0"], wp["a2"], wp["g2"]]
    if has_vres:
        in_specs += [tok(d), vec, _full((d, LANE)), _full((LANE, d))]
        args += [v_first, wp["v0"], wp["v1"], wp["v2"]]
    r, lw, k, v, a, g = pl.pallas_call(
        functools.partial(_rwkv_prep_kernel, has_vres),
        grid=(bsz, ns),
        in_specs=in_specs,
        out_specs=[tok(d)] * 6,
        out_shape=[jax.ShapeDtypeStruct((n_tok, d), F32)] * 6,
        scratch_shapes=[pltpu.VMEM((1, RWKV_IN), F32)],
        compiler_params=_params("parallel", "arbitrary"),
        name="rwkv_prep",
    )(*args)

    ts = min(RWKV_TILE, seq)
    nt = seq // ts
    width = RWKV_SCAN_HEADS * RWKV_HEAD
    blk = pl.BlockSpec((ts, width), lambda b, hp, i: (b * nt + i, hp))
    par = pl.BlockSpec((1, width), lambda b, hp, i: (0, hp))
    o = pl.pallas_call(
        _rwkv_scan_kernel,
        grid=(bsz, RWKV_HEADS // RWKV_SCAN_HEADS, nt),
        in_specs=[blk] * 6 + [par] * 5,
        out_specs=blk,
        out_shape=jax.ShapeDtypeStruct((n_tok, d), BF16),
        scratch_shapes=[pltpu.VMEM((RWKV_SCAN_HEADS // RWKV_PACK, RWKV_HEAD, RWKV_PACK * RWKV_HEAD), F32),
                        pltpu.VMEM((ts, width), F32)],
        compiler_params=_params("parallel", "parallel", "arbitrary"),
        name="rwkv_scan",
    )(r, lw, k, v, a, g, wp["k_k"], wp["k_a"], wp["r_k"], wp["ln_g"], wp["ln_b"])
    return o, v


def _ssm_prep_kernel(x_ref, gmix_ref, w_ref, cw_ref, cb_ref, dtb_ref,
                     z_out, xs_out, b_out, c_out, dt_out, carry_ref):
    @pl.when(pl.program_id(1) == 0)
    def _():
        carry_ref[...] = jnp.zeros_like(carry_ref)

    h = _rms(x_ref[...], gmix_ref[...], NORM_EPS).astype(BF16)
    tm = h.shape[0]
    gw = SSM_CONV_GROUP
    top_row = _iota2((SUBLANE, gw), 0)
    for j in range(SSM_CONV_DIM // gw):
        cs = slice(gw * j, gw * (j + 1))
        xbc = _dot(h, w_ref[:, SSM_DIM + gw * j:SSM_DIM + gw * (j + 1)])
        if gw * j < SSM_DIM:
            z_out[:, cs] = _dot(h, w_ref[:, cs])
        else:
            dt_out[...] = _softplus(_dot(h, w_ref[:, SSM_DIM + SSM_CONV_DIM:]) + dtb_ref[...])
        prev = carry_ref[:, cs]
        carry_ref[:, cs] = xbc[tm - SUBLANE:, :]
        acc = xbc * cw_ref[SSM_CONV - 1:SSM_CONV, cs] + cb_ref[:, cs]
        for s in range(1, SSM_CONV):
            rolled = pltpu.roll(xbc, s, 0)
            top = jnp.where(top_row < s, pltpu.roll(prev, s, 0), rolled[:SUBLANE])
            shifted = jnp.concatenate([top, rolled[SUBLANE:]], axis=0)
            acc = acc + shifted * cw_ref[SSM_CONV - 1 - s:SSM_CONV - s, cs]
        act = acc * _sigmoid(acc)
        if gw * j < SSM_DIM:
            xs_out[:, cs] = act
        else:
            gn = SSM_GROUPS * SSM_STATE
            b_out[...] = act[:, :gn].astype(BF16)
            c_out[...] = act[:, gn:].astype(BF16)


def _ssd_kernel(xs_ref, b_ref, c_ref, dt_ref, z_ref, alog_ref, d_ref, ng_ref, o_ref, state_ref, y_ref):
    @pl.when(pl.program_id(1) == 0)
    def _():
        state_ref[...] = jnp.zeros_like(state_ref)

    L = xs_ref.shape[0]
    row = _iota2((L, L), 0)
    col = _iota2((L, L), 1)
    lower = row >= col
    tril = lower.astype(F32)
    eye = (_iota2((LANE, LANE), 0) == _iota2((LANE, LANE), 1)).astype(F32)

    dt = dt_ref[...]
    da = dt * (-jnp.exp(alog_ref[...]))
    acs = _dot_exact(tril, da)
    acs_t = _dot_nt_exact(eye, acs)
    dt_t = _dot_nt_exact(eye, dt)
    first = _iota2((1, LANE), 1) < SSM_HEAD

    for g in range(SSM_GROUPS):
        gs = slice(SSM_STATE * g, SSM_STATE * (g + 1))
        bm = b_ref[:, gs]
        cm = c_ref[:, gs]
        cb16 = _dot_nt(cm, bm).astype(BF16)
        bm_t = _dot_nt(eye, bm)
        for e in range(0, SSM_HPG, 2):
            pair = (SSM_HPG * g + e) // 2
            ps = slice(LANE * pair, LANE * (pair + 1))
            xs = xs_ref[:, ps]
            xs16 = xs.astype(BF16)
            st = state_ref[pair]
            y_in = _dot(cm, st)
            ys, sts = [], []
            for hd in (2 * pair, 2 * pair + 1):
                a_col = jnp.broadcast_to(acs[:, hd:hd + 1], (L, LANE))
                a_row = acs_t[hd:hd + 1, :]
                dt_row = dt_t[hd:hd + 1, :]
                tot = acs[L - 1:L, hd:hd + 1]
                seg = jnp.concatenate([jnp.exp(a_col - a_row[:, LANE * j:LANE * (j + 1)])
                                       for j in range(L // LANE)], axis=1)
                w = cb16 * jnp.where(lower, seg, 0.0).astype(BF16) * dt_row.astype(BF16)
                ys.append(_dot(w, xs16) + y_in * jnp.exp(a_col))
                sts.append(st * jnp.exp(tot) + _dot(bm_t * (dt_row * jnp.exp(tot - a_row)), xs16))
            y_ref[:, ps] = jnp.where(first, ys[0], ys[1]) + xs * d_ref[:, ps]
            state_ref[pair] = jnp.where(first, sts[0], sts[1])

    z = z_ref[...]
    y = y_ref[...] * (z * _sigmoid(z))
    gw = SSM_DIM // SSM_GROUPS
    for g in range(SSM_GROUPS):
        gs = slice(gw * g, gw * (g + 1))
        yg = y[:, gs]
        yg = yg * lax.rsqrt(jnp.mean(yg * yg, axis=-1, keepdims=True) + SSM_NORM_EPS)
        o_ref[:, gs] = (yg * ng_ref[:, gs]).astype(BF16)


def _ssm_branch(x2, bsz, seq, gmix, wp):
    n_tok = x2.shape[0]
    tm = min(TOK_TILE, seq)
    ns = seq // tm
    gn = SSM_GROUPS * SSM_STATE
    tok = lambda w: pl.BlockSpec((tm, w), lambda b, i: (b * ns + i, 0))
    z, xs, bm, cm, dt = pl.pallas_call(
        _ssm_prep_kernel,
        grid=(bsz, ns),
        in_specs=[tok(D_MODEL), _full((1, D_MODEL)), _full(wp["w_ssm"].shape), _full((SSM_CONV, SSM_CONV_DIM)),
                  _full((1, SSM_CONV_DIM)), _full((1, LANE))],
        out_specs=[tok(SSM_DIM), tok(SSM_DIM), tok(gn), tok(gn), tok(LANE)],
        out_shape=[jax.ShapeDtypeStruct((n_tok, SSM_DIM), F32), jax.ShapeDtypeStruct((n_tok, SSM_DIM), F32),
                   jax.ShapeDtypeStruct((n_tok, gn), BF16), jax.ShapeDtypeStruct((n_tok, gn), BF16),
                   jax.ShapeDtypeStruct((n_tok, LANE), F32)],
        scratch_shapes=[pltpu.VMEM((SUBLANE, SSM_CONV_DIM), F32)],
        compiler_params=_params("parallel", "arbitrary"),
        name="ssm_prep",
    )(x2, gmix, wp["w_ssm"], wp["conv_w"], wp["conv_b"], wp["dt_bias"])

    assert seq % SSM_CHUNK == 0
    nc = seq // SSM_CHUNK
    ch = lambda w: pl.BlockSpec((SSM_CHUNK, w), lambda b, c: (b * nc + c, 0))
    return pl.pallas_call(
        _ssd_kernel,
        grid=(bsz, nc),
        in_specs=[ch(SSM_DIM), ch(gn), ch(gn), ch(LANE), ch(SSM_DIM), _full((1, LANE)), _full((1, SSM_DIM)),
                  _full((1, SSM_DIM))],
        out_specs=ch(SSM_DIM),
        out_shape=jax.ShapeDtypeStruct((n_tok, SSM_DIM), BF16),
        scratch_shapes=[pltpu.VMEM((SSM_HEADS // 2, SSM_STATE, 2 * SSM_HEAD), F32),
                        pltpu.VMEM((SSM_CHUNK, SSM_DIM), F32)],
        compiler_params=_params("parallel", "arbitrary"),
        name="ssd",
    )(xs, bm, cm, dt, z, wp["a_log"], wp["d_skip"], wp["norm_g"])


def _merge_kernel(x_ref, gmix_ref, wg_ref, om_ref, or_ref, os_ref, wm_ref, wr_ref, ws_ref, wo_ref, o_ref):
    x = x_ref[...]
    h = _rms(x, gmix_ref[...], NORM_EPS).astype(BF16)
    d = D_MODEL
    merged = _sigmoid(_dot(h, wg_ref[:, :d])) * _dot(om_ref[...], wm_ref[...])
    merged = merged + _sigmoid(_dot(h, wg_ref[:, d:2 * d])) * _dot(or_ref[...], wr_ref[...])
    merged = merged + _sigmoid(_dot(h, wg_ref[:, 2 * d:])) * _dot(os_ref[...], ws_ref[...])
    o_ref[...] = x + _dot(merged, wo_ref[...])


def _ffn_kernel(x_ref, g_ref, w1_ref, w2_ref, o_ref, h_ref):
    j = pl.program_id(1)

    @pl.when(j == 0)
    def _():
        x = x_ref[...]
        h_ref[...] = _rms(x, g_ref[...], NORM_EPS).astype(BF16)
        o_ref[...] = x

    u = jnp.maximum(_dot(h_ref[...], w1_ref[...]), 0.0)
    o_ref[...] += _dot(u * u, w2_ref[...])


def _merge_ffn(x2, gmix, wp, o_mla, o_rwkv, o_ssm):
    n_tok = x2.shape[0]
    tm = min(TOK_TILE, n_tok)
    tok = lambda w: pl.BlockSpec((tm, w), lambda i: (i, 0))
    x2 = pl.pallas_call(
        _merge_kernel,
        grid=(n_tok // tm,),
        in_specs=[tok(D_MODEL), _full((1, D_MODEL)), _full((D_MODEL, GATE_IN)), tok(o_mla.shape[1]),
                  tok(o_rwkv.shape[1]), tok(o_ssm.shape[1]), _full(wp["w_br_mla"].shape),
                  _full(wp["w_br_rwkv"].shape), _full(wp["w_br_ssm"].shape), _full((D_MODEL, D_MODEL))],
        out_specs=tok(D_MODEL),
        out_shape=jax.ShapeDtypeStruct((n_tok, D_MODEL), F32),
        compiler_params=_params("parallel"),
        name="merge",
    )(x2, gmix, wp["w_gate"], o_mla, o_rwkv, o_ssm, wp["w_br_mla"], wp["w_br_rwkv"], wp["w_br_ssm"], wp["w_out"])

    tf = min(FFN_TOK_TILE, n_tok)
    ff = FFN_FF_TILE
    return pl.pallas_call(
        _ffn_kernel,
        grid=(n_tok // tf, D_FF // ff),
        in_specs=[pl.BlockSpec((tf, D_MODEL), lambda i, j: (i, 0)), _full((1, D_MODEL)),
                  pl.BlockSpec((D_MODEL, ff), lambda i, j: (0, j)), pl.BlockSpec((ff, D_MODEL), lambda i, j: (j, 0))],
        out_specs=pl.BlockSpec((tf, D_MODEL), lambda i, j: (i, 0)),
        out_shape=jax.ShapeDtypeStruct((n_tok, D_MODEL), F32),
        scratch_shapes=[pltpu.VMEM((tf, D_MODEL), BF16)],
        compiler_params=_params("parallel", "arbitrary"),
        name="ffn",
    )(x2, wp["norm_ffn_g"], wp["w_ff1"], wp["w_ff2"])


def _pad_cols(w, width):
    return jnp.pad(w, ((0, 0), (0, width - w.shape[1])))


def _pad_rows(w, height, before=0):
    return jnp.pad(w, ((before, height - before - w.shape[0]), (0, 0)))


def _row(v, width=None):
    v = v.reshape(1, -1).astype(F32)
    return v if width is None else _pad_cols(v, width)


def _layer_weights(l, p):
    w_in = p["w_in"][l]
    o_r = MLA_IN
    o_s = o_r + RWKV_IN
    o_g = o_s + SSM_IN
    zc = lambda n: jnp.zeros((D_MODEL, n), w_in.dtype)
    w_mla = jnp.concatenate([w_in[:, :MLA_Q_RANK + MLA_KV_RANK], zc(MLA_NOPE), w_in[:, MLA_Q_RANK + MLA_KV_RANK:MLA_IN],
                             zc(LANE - MLA_QK)], axis=1)
    half = MLA_ROPE // 2

    def slot(t, rotate):
        lo, hi = t[..., MLA_NOPE:MLA_NOPE + half], t[..., MLA_NOPE + half:]
        parts = [jnp.zeros_like(t[..., :MLA_NOPE]), -hi if rotate == "signed" else hi, lo] if rotate else [t]
        t = jnp.concatenate(parts, axis=-1)
        return jnp.pad(t, [(0, 0)] * (t.ndim - 1) + [(0, LANE - MLA_QK)])

    w_uq = p["mla_w_uq"][l].reshape(MLA_Q_RANK, MLA_HEADS, MLA_QK)
    w_uq = jnp.concatenate([slot(w_uq, None).reshape(MLA_Q_RANK, MLA_HEADS * LANE),
                            slot(w_uq, "signed").reshape(MLA_Q_RANK, MLA_HEADS * LANE)], axis=1)
    q_head_g = p["mla_q_head_g"][l].astype(F32)
    q_head_g = jnp.stack([slot(q_head_g, None), slot(q_head_g, "moved")])
    w_ukv = p["mla_w_ukv"][l].reshape(MLA_KV_RANK, MLA_HEADS, MLA_NOPE + MLA_V)
    w_uk = jnp.pad(w_ukv[:, :, :MLA_NOPE], ((0, 0), (0, 0), (0, LANE - MLA_NOPE))).reshape(MLA_KV_RANK, MLA_HEADS * LANE)
    w_uv = jnp.pad(w_ukv[:, :, MLA_NOPE:], ((0, 0), (0, 0), (0, LANE - MLA_V))).reshape(MLA_KV_RANK, MLA_HEADS * LANE)
    w_ssm = _pad_cols(w_in[:, o_s:o_g], SSM_DIM + SSM_CONV_DIM + LANE)
    wp = {
        "w_mla": w_mla.astype(BF16), "w_uq": w_uq.astype(BF16), "w_uk": w_uk.astype(BF16), "w_uv": w_uv.astype(BF16),
        "q_norm_g": _row(p["mla_q_norm_g"][l]), "kv_norm_g": _row(p["mla_kv_norm_g"][l]),
        "q_head_g": q_head_g, "k_head_g": _row(p["mla_k_head_g"][l], LANE),
        "w_rwkv": w_in[:, o_r:o_s].astype(BF16), "mu": _row(p["rwkv_mu"][l]), "w0": _row(p["rwkv_w0"][l]),
        "w2": _pad_rows(p["rwkv_w2"][l], LANE).astype(BF16), "a0": _row(p["rwkv_a0"][l]),
        "a2": _pad_rows(p["rwkv_a2"][l], LANE, before=RWKV_W_RANK).astype(BF16), "g2": p["rwkv_g2"][l].astype(BF16),
        "k_k": _row(p["rwkv_k_k"][l]), "k_a": _row(p["rwkv_k_a"][l]), "r_k": _row(p["rwkv_r_k"][l]),
        "ln_g": _row(p["rwkv_ln_g"][l]), "ln_b": _row(p["rwkv_ln_b"][l]),
        "w_ssm": w_ssm.astype(BF16), "conv_w": p["ssm_conv_w"][l].astype(F32), "conv_b": _row(p["ssm_conv_b"][l]),
        "dt_bias": _row(p["ssm_dt_bias"][l], LANE), "a_log": _row(p["ssm_a_log"][l], LANE),
        "d_skip": _row(jnp.repeat(p["ssm_d"][l], SSM_HEAD)), "norm_g": _row(p["ssm_norm_g"][l]),
        "w_gate": w_in[:, o_g:].astype(BF16), "w_br_mla": p["w_br_mla"][l].astype(BF16),
        "w_br_rwkv": p["w_br_rwkv"][l].astype(BF16), "w_br_ssm": p["w_br_ssm"][l].astype(BF16),
        "w_out": p["w_out"][l].astype(BF16), "norm_ffn_g": _row(p["norm_ffn_g"][l]),
        "w_ff1": p["w_ff1"][l].astype(BF16), "w_ff2": p["w_ff2"][l].astype(BF16),
    }
    if l > 0:
        wp["v0"] = _row(p["rwkv_v0"][l - 1])
        wp["v1"] = _pad_cols(p["rwkv_v1"][l - 1], LANE).astype(BF16)
        wp["v2"] = _pad_rows(p["rwkv_v2"][l - 1], LANE).astype(BF16)
    return wp


def kernel(x, positions, norm_mix_g, w_in, mla_q_norm_g, mla_kv_norm_g, mla_w_uq, mla_w_ukv, mla_q_head_g, mla_k_head_g, rwkv_mu, rwkv_w0, rwkv_w2, rwkv_a0, rwkv_a2, rwkv_g2, rwkv_v0, rwkv_v1, rwkv_v2, rwkv_k_k, rwkv_k_a, rwkv_r_k, rwkv_ln_g, rwkv_ln_b, ssm_conv_w, ssm_conv_b, ssm_dt_bias, ssm_a_log, ssm_d, ssm_norm_g, w_br_mla, w_br_rwkv, w_br_ssm, w_out, norm_ffn_g, w_ff1, w_ff2):
    p = dict(w_in=w_in, mla_q_norm_g=mla_q_norm_g, mla_kv_norm_g=mla_kv_norm_g, mla_w_uq=mla_w_uq,
             mla_w_ukv=mla_w_ukv, mla_q_head_g=mla_q_head_g, mla_k_head_g=mla_k_head_g, rwkv_mu=rwkv_mu,
             rwkv_w0=rwkv_w0, rwkv_w2=rwkv_w2, rwkv_a0=rwkv_a0, rwkv_a2=rwkv_a2, rwkv_g2=rwkv_g2, rwkv_v0=rwkv_v0,
             rwkv_v1=rwkv_v1, rwkv_v2=rwkv_v2, rwkv_k_k=rwkv_k_k, rwkv_k_a=rwkv_k_a, rwkv_r_k=rwkv_r_k,
             rwkv_ln_g=rwkv_ln_g, rwkv_ln_b=rwkv_ln_b, ssm_conv_w=ssm_conv_w, ssm_conv_b=ssm_conv_b,
             ssm_dt_bias=ssm_dt_bias, ssm_a_log=ssm_a_log, ssm_d=ssm_d, ssm_norm_g=ssm_norm_g, w_br_mla=w_br_mla,
             w_br_rwkv=w_br_rwkv, w_br_ssm=w_br_ssm, w_out=w_out, norm_ffn_g=norm_ffn_g, w_ff1=w_ff1, w_ff2=w_ff2)
    bsz, seq, d_model = x.shape
    depth = w_in.shape[0]
    x2 = x.reshape(bsz * seq, d_model)
    tables = _rope_tables(positions)
    v_first = None
    for l in range(depth):
        wp = _layer_weights(l, p)
        gmix = _row(norm_mix_g[l])
        o_mla = _mla_branch(x2, bsz, seq, gmix, wp, tables)
        o_rwkv, v = _rwkv_branch(x2, bsz, seq, gmix, wp, v_first)
        if l == 0:
            v_first = v
        o_ssm = _ssm_branch(x2, bsz, seq, gmix, wp)
        x2 = _merge_ffn(x2, gmix, wp, o_mla, o_rwkv, o_ssm)
    return x2.reshape(bsz, seq, d_model)
```
